```python
import jax, jax.numpy as jnp
from jax import lax
import numpy as np


D_MODEL = 4096
BATCH = 4
SEQ = 2048
DEPTH = 4
DEC_BATCH = 128
DEC_SEQ = 1
PAST_LEN = 16384
PAGE_SIZE = 128

N_META = 16
N_MIXERS = 2
N_POOL_LAYERS = (DEPTH + 1) // 2
N_GDN_LAYERS = DEPTH // 2
EXPAND = 2
E_POOL = EXPAND * D_MODEL
POOL_WINDOWS = (2, 4, 8, 16)
N_POOL_GROUPS = len(POOL_WINDOWS)
G_POOL = E_POOL // N_POOL_GROUPS
POOL_CTX = max(POOL_WINDOWS) - 1
HEAD_DIM = 128
N_K_HEADS = D_MODEL // HEAD_DIM
N_V_HEADS = 2 * N_K_HEADS
KEY_DIM = N_K_HEADS * HEAD_DIM
VALUE_DIM = N_V_HEADS * HEAD_DIM
CONV_DIM = 2 * KEY_DIM + VALUE_DIM
CONV_WIDTH = 4
CHUNK = 64
GDN_IN = CONV_DIM + VALUE_DIM + 2 * N_V_HEADS
EPS = 1e-6

kernel_name = 'hybrid_pool_gdn_decoder_step'


def rmsnorm(x, g):
    x32 = x.astype(jnp.float32)
    y = x32 * lax.rsqrt(jnp.mean(x32 * x32, axis=-1, keepdims=True) + EPS)
    return (y * g.astype(jnp.float32)).astype(x.dtype)


def l2norm(x):
    return x * lax.rsqrt(jnp.sum(x * x, axis=-1, keepdims=True) + EPS)


def pool_mixer(h, prev, pos0, w_in, w_grp, scale, w_out):
    B, T, _ = h.shape
    uz = h @ w_in
    u, z = uz[..., :E_POOL], uz[..., E_POOL:]
    ext = jnp.concatenate([prev.astype(u.dtype), u], axis=1)
    cs = jnp.cumsum(ext.astype(jnp.float32), axis=1)
    cs = jnp.pad(cs, ((0, 0), (1, 0), (0, 0)))
    pos = (pos0 + jnp.arange(T)).astype(jnp.float32)
    u32 = u.astype(jnp.float32)
    diffs = []
    for gi, w in enumerate(POOL_WINDOWS):
        sl = slice(gi * G_POOL, (gi + 1) * G_POOL)
        lo = POOL_CTX + 1 - w
        wsum = cs[:, POOL_CTX + 1:POOL_CTX + 1 + T, sl] - cs[:, lo:lo + T, sl]
        cnt = jnp.minimum(pos + 1.0, float(w))[None, :, None]
        diffs.append(wsum / cnt - u32[..., sl])
    d = jnp.stack(diffs, axis=2).astype(h.dtype)
    m = jnp.einsum('btkg,kgh->btkh', d, w_grp).reshape(B, T, E_POOL) * scale
    out = (m * jax.nn.silu(z)) @ w_out
    return out, ext[:, -POOL_CTX:]


def causal_conv(xc, prev, w):
    T = xc.shape[1]
    ext = jnp.concatenate([prev.astype(xc.dtype), xc], axis=1)
    acc = ext[:, 0:T] * w[0]
    for i in range(1, CONV_WIDTH):
        acc = acc + ext[:, i:i + T] * w[i]
    return jax.nn.silu(acc), ext[:, -(CONV_WIDTH - 1):]


def gdn_recurrent(q, k, v, g, beta, S0):
    def step(S, inp):
        q_t, k_t, v_t, g_t, b_t = inp
        S = S * jnp.exp(g_t)[..., None, None]
        kv = jnp.einsum('bhd,bhde->bhe', k_t, S)
        delta = (v_t - kv) * b_t[..., None]
        S = S + k_t[..., :, None] * delta[..., None, :]
        o = jnp.einsum('bhd,bhde->bhe', q_t, S)
        return S, o
    xs = (jnp.moveaxis(q, 1, 0), jnp.moveaxis(k, 1, 0), jnp.moveaxis(v, 1, 0),
          jnp.moveaxis(g, 1, 0), jnp.moveaxis(beta, 1, 0))
    S, o = lax.scan(step, S0, xs)
    return jnp.moveaxis(o, 0, 1), S


def gdn_chunked(q, k, v, g, beta, S0, front):
    B, T, H, DK = q.shape
    DV = v.shape[-1]
    back = (-(front + T)) % CHUNK

    def pad(a):
        return jnp.pad(a, ((0, 0), (front, back)) + ((0, 0),) * (a.ndim - 2))

    q, k, v, g, beta = pad(q), pad(k), pad(v), pad(g), pad(beta)
    n = (front + T + back) // CHUNK

    def to_chunks(a):
        a = a.reshape((B, n, CHUNK) + a.shape[2:])
        return jnp.moveaxis(a, 3, 1)

    q, k, v, g, beta = to_chunks(q), to_chunks(k), to_chunks(v), to_chunks(g), to_chunks(beta)
    idx = jnp.arange(CHUNK)
    tril = idx[:, None] >= idx[None, :]
    strict = idx[:, None] > idx[None, :]
    gc = jnp.cumsum(g, axis=-1)
    L = jnp.exp(jnp.where(tril, gc[..., :, None] - gc[..., None, :], -jnp.inf))
    kb = k * beta[..., None]
    A = jnp.where(strict, jnp.einsum('bhncd,bhnsd->bhncs', kb, k) * L, 0.0)
    eye = jnp.eye(CHUNK, dtype=jnp.float32)
    rhs = jnp.concatenate([v * beta[..., None], kb * jnp.exp(gc)[..., None]], axis=-1)
    sol = lax.linalg.triangular_solve(A + eye, rhs, left_side=True, lower=True,
                                      unit_diagonal=True)
    u, w = sol[..., :DV], sol[..., DV:]
    attn = jnp.einsum('bhncd,bhnsd->bhncs', q, k) * L
    qd = q * jnp.exp(gc)[..., None]
    kd = k * jnp.exp(gc[..., -1:] - gc)[..., None]
    gl = jnp.exp(gc[..., -1])

    def step(S, inp):
        u_c, w_c, qd_c, kd_c, a_c, gl_c = inp
        v_new = u_c - jnp.einsum('bhcd,bhde->bhce', w_c, S)
        o = jnp.einsum('bhcd,bhde->bhce', qd_c, S) + jnp.einsum('bhcs,bhse->bhce', a_c, v_new)
        S = S * gl_c[..., None, None] + jnp.einsum('bhcd,bhce->bhde', kd_c, v_new)
        return S, o

    xs = (jnp.moveaxis(u, 2, 0), jnp.moveaxis(w, 2, 0), jnp.moveaxis(qd, 2, 0),
          jnp.moveaxis(kd, 2, 0), jnp.moveaxis(attn, 2, 0), jnp.moveaxis(gl, 2, 0))
    S, o = lax.scan(step, S0, xs)
    o = jnp.transpose(o, (1, 0, 3, 2, 4)).reshape(B, n * CHUNK, H, DV)
    return o[:, front:front + T], S


def gdn_mixer(h, conv_prev, S0, chunked, w_in, conv_w, A_log, dt_bias, norm_g, w_out):
    B, T, _ = h.shape
    f32 = jnp.float32
    p = h @ w_in
    qkv = p[..., :CONV_DIM]
    z = p[..., CONV_DIM:CONV_DIM + VALUE_DIM]
    b = p[..., CONV_DIM + VALUE_DIM:CONV_DIM + VALUE_DIM + N_V_HEADS]
    a = p[..., CONV_DIM + VALUE_DIM + N_V_HEADS:]
    qkv, conv_new = causal_conv(qkv, conv_prev, conv_w)
    q = qkv[..., :KEY_DIM].reshape(B, T, N_K_HEADS, HEAD_DIM).astype(f32)
    k = qkv[..., KEY_DIM:2 * KEY_DIM].reshape(B, T, N_K_HEADS, HEAD_DIM).astype(f32)
    v = qkv[..., 2 * KEY_DIM:].reshape(B, T, N_V_HEADS, HEAD_DIM).astype(f32)
    q = l2norm(q) * (HEAD_DIM ** -0.5)
    k = l2norm(k)
    rep = N_V_HEADS // N_K_HEADS
    q = jnp.repeat(q, rep, axis=2)
    k = jnp.repeat(k, rep, axis=2)
    beta = jax.nn.sigmoid(b.astype(f32))
    g = -jnp.exp(A_log.astype(f32)) * jax.nn.softplus(a.astype(f32) + dt_bias.astype(f32))
    S0 = S0.astype(f32)
    if chunked:
        o, S = gdn_chunked(q, k, v, g, beta, S0, (-N_META) % CHUNK)
    else:
        o, S = gdn_recurrent(q, k, v, g, beta, S0)
    o = rmsnorm(o, norm_g).astype(h.dtype).reshape(B, T, VALUE_DIM)
    out = (o * jax.nn.silu(z)) @ w_out
    return out, conv_new, S


def setup_inputs(seed: int = 0) -> dict:
    key = jax.random.key(seed)
    ks = jax.random.split(key, 20)
    f32 = jnp.float32

    def nrm(k, shape, scale):
        return jax.random.normal(k, shape, f32) * scale

    dt = jnp.exp(jax.random.uniform(ks[15], (N_GDN_LAYERS, N_V_HEADS), f32,
                                    np.log(1e-3), np.log(1e-1)))
    return {
        'x_prompt': nrm(ks[0], (BATCH, SEQ, D_MODEL), 1.0),
        'x_sample': nrm(ks[1], (DEC_BATCH, DEC_SEQ, D_MODEL), 1.0),
        'state_pool': nrm(ks[2], (N_POOL_LAYERS, DEC_BATCH, POOL_CTX, E_POOL), 1.0),
        'state_conv': nrm(ks[3], (N_GDN_LAYERS, DEC_BATCH, CONV_WIDTH - 1, CONV_DIM), 1.0),
        'state_ssm': nrm(ks[4], (N_GDN_LAYERS, DEC_BATCH, N_V_HEADS, HEAD_DIM, HEAD_DIM), 0.1),
        'meta_tokens': nrm(ks[5], (N_META, D_MODEL), 1.0),
        'norm_g': 1.0 + nrm(ks[6], (DEPTH, D_MODEL), 0.02),
        'final_norm_g': 1.0 + nrm(ks[7], (D_MODEL,), 0.02),
        'pool_w_in': nrm(ks[8], (N_POOL_LAYERS, D_MODEL, 2 * E_POOL), D_MODEL ** -0.5),
        'pool_w_grp': nrm(ks[9], (N_POOL_LAYERS, N_POOL_GROUPS, G_POOL, G_POOL), G_POOL ** -0.5),
        'pool_scale': 1.0 + nrm(ks[10], (N_POOL_LAYERS, E_POOL), 0.02),
        'pool_w_out': nrm(ks[11], (N_POOL_LAYERS, E_POOL, D_MODEL), E_POOL ** -0.5),
        'gdn_w_in': nrm(ks[12], (N_GDN_LAYERS, D_MODEL, GDN_IN), D_MODEL ** -0.5),
        'gdn_conv_w': nrm(ks[13], (N_GDN_LAYERS, CONV_WIDTH, CONV_DIM), CONV_WIDTH ** -0.5),
        'gdn_A_log': jnp.log(jax.random.uniform(ks[14], (N_GDN_LAYERS, N_V_HEADS), f32, 1.0, 16.0)),
        'gdn_dt_bias': dt + jnp.log(-jnp.expm1(-dt)),
        'gdn_norm_g': 1.0 + nrm(ks[16], (N_GDN_LAYERS, HEAD_DIM), 0.02),
        'gdn_w_out': nrm(ks[17], (N_GDN_LAYERS, VALUE_DIM, D_MODEL), VALUE_DIM ** -0.5),
    }


def reference(x_prompt, x_sample, state_pool, state_conv, state_ssm, meta_tokens, norm_g,
              final_norm_g, pool_w_in, pool_w_grp, pool_scale, pool_w_out, gdn_w_in,
              gdn_conv_w, gdn_A_log, gdn_dt_bias, gdn_norm_g, gdn_w_out):
    dt = x_prompt.dtype
    B = x_prompt.shape[0]
    meta = jnp.broadcast_to(meta_tokens.astype(dt)[None], (B, N_META, D_MODEL))
    hp = jnp.concatenate([meta, x_prompt], axis=1)
    hs = x_sample
    pool_p, conv_p, ssm_p, pool_s, conv_s, ssm_s = [], [], [], [], [], []
    for i in range(DEPTH):
        j = i // N_MIXERS
        np_ = rmsnorm(hp, norm_g[i])
        ns_ = rmsnorm(hs, norm_g[i])
        if i % N_MIXERS == 0:
            zp = jnp.zeros((B, POOL_CTX, E_POOL), dt)
            op, sp = pool_mixer(np_, zp, 0, pool_w_in[j], pool_w_grp[j], pool_scale[j], pool_w_out[j])
            os_, ss = pool_mixer(ns_, state_pool[j], PAST_LEN, pool_w_in[j], pool_w_grp[j],
                                 pool_scale[j], pool_w_out[j])
            pool_p.append(sp)
            pool_s.append(ss.astype(state_pool.dtype))
        else:
            zc = jnp.zeros((B, CONV_WIDTH - 1, CONV_DIM), dt)
            zs = jnp.zeros((B, N_V_HEADS, HEAD_DIM, HEAD_DIM), jnp.float32)
            op, cp, Sp = gdn_mixer(np_, zc, zs, True, gdn_w_in[j], gdn_conv_w[j], gdn_A_log[j],
                                   gdn_dt_bias[j], gdn_norm_g[j], gdn_w_out[j])
            os_, cs_, Ss = gdn_mixer(ns_, state_conv[j], state_ssm[j], False, gdn_w_in[j],
                                     gdn_conv_w[j], gdn_A_log[j], gdn_dt_bias[j], gdn_norm_g[j],
                                     gdn_w_out[j])
            conv_p.append(cp)
            ssm_p.append(Sp.astype(dt))
            conv_s.append(cs_.astype(state_conv.dtype))
            ssm_s.append(Ss.astype(state_ssm.dtype))
        hp = hp + op
        hs = hs + os_
    y_prompt = rmsnorm(hp, final_norm_g)[:, N_META:]
    y_sample = rmsnorm(hs, final_norm_g)
    return (y_prompt, y_sample, jnp.stack(pool_p), jnp.stack(conv_p), jnp.stack(ssm_p),
            jnp.stack(pool_s), jnp.stack(conv_s), jnp.stack(ssm_s))
```

```python
import functools

import jax
import jax.numpy as jnp
from jax import lax
from jax.experimental import pallas as pl
from jax.experimental.pallas import tpu as pltpu

D_MODEL = 4096
N_META = 16
PAST_LEN = 16384
POOL_WINDOWS = (2, 4, 8, 16)
POOL_CTX = max(POOL_WINDOWS) - 1
HEAD_DIM = 128
CONV_WIDTH = 4
CHUNK = 64
EPS = 1e-6
FRONT = (-N_META) % CHUNK
QUAD = 4
NEUMANN_LEVELS = 6

LANE = 128
VMEM_LIMIT = 56 * 1024 * 1024

f32 = jnp.float32
bf16 = jnp.bfloat16


def _params(sem):
    return pltpu.CompilerParams(dimension_semantics=sem, vmem_limit_bytes=VMEM_LIMIT)


def _silu(x):
    return x * jax.nn.sigmoid(x)


def _rmsnorm_kernel(x_ref, g_ref, o_ref):
    x = x_ref[...]
    y = x * lax.rsqrt(jnp.mean(x * x, axis=-1, keepdims=True) + EPS)
    o_ref[...] = (y * g_ref[...]).astype(o_ref.dtype)


def _rmsnorm(x, g, out_dtype, tm):
    m, d = x.shape
    return pl.pallas_call(
        _rmsnorm_kernel,
        grid=(m // tm,),
        in_specs=[pl.BlockSpec((tm, d), lambda i: (i, 0)),
                  pl.BlockSpec((1, d), lambda i: (0, 0))],
        out_specs=pl.BlockSpec((tm, d), lambda i: (i, 0)),
        out_shape=jax.ShapeDtypeStruct((m, d), out_dtype),
        compiler_params=_params(("parallel",)),
        name="rmsnorm",
    )(x, g.reshape(1, d))


def _final_norm_prompt(h3, g, seq):
    b, t, d = h3.shape
    skip = (t - seq) // CHUNK
    return pl.pallas_call(
        _rmsnorm_kernel,
        grid=(b, seq // CHUNK),
        in_specs=[pl.BlockSpec((None, CHUNK, d), lambda i, j: (i, j + skip, 0)),
                  pl.BlockSpec((1, d), lambda i, j: (0, 0))],
        out_specs=pl.BlockSpec((None, CHUNK, d), lambda i, j: (i, j, 0)),
        out_shape=jax.ShapeDtypeStruct((b, seq, d), h3.dtype),
        compiler_params=_params(("parallel", "parallel")),
        name="final_norm",
    )(h3, g.reshape(1, d))


def _mm_kernel(*refs, epilogue):
    x_ref, w_ref = refs[0], refs[1]
    o_ref = refs[-1]
    acc = jnp.dot(x_ref[...], w_ref[...], preferred_element_type=f32)
    if epilogue == "plain":
        o_ref[...] = acc.astype(o_ref.dtype)
    elif epilogue == "gate":
        scale_ref, z_ref = refs[2], refs[3]
        o_ref[...] = (acc * scale_ref[...] * _silu(z_ref[...])).astype(o_ref.dtype)
    elif epilogue == "residual":
        h_ref = refs[2]
        o_ref[...] = h_ref[...] + acc
    else:
        raise ValueError(epilogue)


def _mm_tiles(m):
    tm = 768 if m % 768 == 0 else m
    return tm, 512


def _matmul(x, w, col0, ncols, out_dtype=f32, residual=None):
    m, k = x.shape
    tm, tn = _mm_tiles(m)
    tn = min(tn, ncols)
    off = col0 // tn
    assert col0 % tn == 0 and ncols % tn == 0 and m % tm == 0
    in_specs = [pl.BlockSpec((tm, k), lambda i, j: (i, 0)),
                pl.BlockSpec((k, tn), lambda i, j: (0, j + off))]
    args = [x, w]
    if residual is not None:
        in_specs.append(pl.BlockSpec((tm, tn), lambda i, j: (i, j)))
        args.append(residual)
    return pl.pallas_call(
        functools.partial(_mm_kernel, epilogue="plain" if residual is None else "residual"),
        grid=(m // tm, ncols // tn),
        in_specs=in_specs,
        out_specs=pl.BlockSpec((tm, tn), lambda i, j: (i, j)),
        out_shape=jax.ShapeDtypeStruct((m, ncols), out_dtype),
        compiler_params=_params(("parallel", "arbitrary")),
        name="matmul",
    )(*args)


def _grouped_matmul_gate(d, w_grp, scale, z):
    m, e = d.shape
    ng, gk, gn = w_grp.shape
    tm, tn = _mm_tiles(m)
    nb = gn // tn
    return pl.pallas_call(
        functools.partial(_mm_kernel, epilogue="gate"),
        grid=(m // tm, ng, nb),
        in_specs=[pl.BlockSpec((tm, gk), lambda i, g, j: (i, g)),
                  pl.BlockSpec((None, gk, tn), lambda i, g, j: (g, 0, j)),
                  pl.BlockSpec((1, tn), lambda i, g, j: (0, g * nb + j)),
                  pl.BlockSpec((tm, tn), lambda i, g, j: (i, g * nb + j))],
        out_specs=pl.BlockSpec((tm, tn), lambda i, g, j: (i, g * nb + j)),
        out_shape=jax.ShapeDtypeStruct((m, e), bf16),
        compiler_params=_params(("parallel", "arbitrary", "arbitrary")),
        name="grouped_matmul_gate",
    )(d, w_grp, scale.reshape(1, e), z)


POOL_HALO = 16
POOL_ROWS = 16
POOL_LANES = 512


def _pool_diff_prompt_kernel(halo_ref, cur_ref, o_ref, ext_ref, *, tm, gsz):
    t = pl.program_id(1)
    gi = (pl.program_id(2) * POOL_LANES) // gsz
    ext_ref[0:POOL_HALO, :] = jnp.where(t > 0, halo_ref[...], 0.0)
    ext_ref[POOL_HALO:POOL_HALO + tm, :] = cur_ref[...]
    for widx, w in enumerate(POOL_WINDOWS):
        @pl.when(gi == widx)
        def _(w=w):
            for r in range(0, tm, POOL_ROWS):
                base = POOL_HALO + r
                cur = ext_ref[base:base + POOL_ROWS, :]
                s = cur
                for i in range(1, w):
                    s = s + ext_ref[base - i:base - i + POOL_ROWS, :]
                pos = lax.broadcasted_iota(jnp.int32, (POOL_ROWS, 1), 0) + (t * tm + r - FRONT)
                cnt = jnp.clip(pos + 1, 1, w).astype(f32)
                o_ref[r:r + POOL_ROWS, :] = (s / cnt - cur).astype(o_ref.dtype)


def _pool_diff_prompt(u3):
    b, t, e = u3.shape
    tm = 528
    assert t % tm == 0 and tm % POOL_HALO == 0 and e % (len(POOL_WINDOWS) * POOL_LANES) == 0
    hb = tm // POOL_HALO
    return pl.pallas_call(
        functools.partial(_pool_diff_prompt_kernel, tm=tm, gsz=e // len(POOL_WINDOWS)),
        grid=(b, t // tm, e // POOL_LANES),
        in_specs=[pl.BlockSpec((None, POOL_HALO, POOL_LANES),
                               lambda i, j, c: (i, jnp.maximum(j * hb - 1, 0), c)),
                  pl.BlockSpec((None, tm, POOL_LANES), lambda i, j, c: (i, j, c))],
        out_specs=pl.BlockSpec((None, tm, POOL_LANES), lambda i, j, c: (i, j, c)),
        out_shape=jax.ShapeDtypeStruct((b, t, e), bf16),
        scratch_shapes=[pltpu.VMEM((POOL_HALO + tm, POOL_LANES), f32)],
        compiler_params=_params(("parallel", "parallel", "parallel")),
        name="pool_diff_prompt",
    )(u3, u3)


def _pool_diff_sample_kernel(st_ref, u_ref, o_ref, *, gsz, lanes):
    gi = (pl.program_id(1) * lanes) // gsz
    for widx, w in enumerate(POOL_WINDOWS):
        @pl.when(gi == widx)
        def _(w=w):
            cur = u_ref[...]
            s = cur
            for i in range(1, w):
                s = s + st_ref[:, POOL_CTX - i, :]
            cnt = float(min(PAST_LEN + 1, w))
            o_ref[...] = (s / cnt - cur).astype(o_ref.dtype)


def _pool_diff_sample(state, u):
    b, _, e = state.shape
    gsz = e // len(POOL_WINDOWS)
    bb, lanes = 16, min(1024, gsz)
    assert b % bb == 0 and gsz % lanes == 0
    return pl.pallas_call(
        functools.partial(_pool_diff_sample_kernel, gsz=gsz, lanes=lanes),
        grid=(b // bb, e // lanes),
        in_specs=[pl.BlockSpec((bb, POOL_CTX, lanes), lambda i, c: (i, 0, c)),
                  pl.BlockSpec((bb, lanes), lambda i, c: (i, c))],
        out_specs=pl.BlockSpec((bb, lanes), lambda i, c: (i, c)),
        out_shape=jax.ShapeDtypeStruct((b, e), bf16),
        compiler_params=_params(("parallel", "parallel")),
        name="pool_diff_sample",
    )(state, u)


CONV_HALO = 8
CONV_ROWS = 24
CONV_LANES = 1024


def _conv_act(acc, kind):
    y = _silu(acc)
    if kind == 2:
        return y
    parts = []
    for h in range(y.shape[1] // HEAD_DIM):
        yh = y[:, h * HEAD_DIM:(h + 1) * HEAD_DIM]
        yh = yh * lax.rsqrt(jnp.sum(yh * yh, axis=-1, keepdims=True) + EPS)
        if kind == 0:
            yh = yh * (HEAD_DIM ** -0.5)
        parts.append(yh)
    return jnp.concatenate(parts, axis=1)


def _conv_kind(c, key_dim):
    return jnp.minimum((c * CONV_LANES) // key_dim, 2)


def _conv_prompt_kernel(halo_ref, cur_ref, w_ref, o_ref, ext_ref, *, tm, key_dim):
    t = pl.program_id(1)
    kind_id = _conv_kind(pl.program_id(2), key_dim)
    ext_ref[0:CONV_HALO, :] = jnp.where(t > 0, halo_ref[...], 0.0)
    ext_ref[CONV_HALO:CONV_HALO + tm, :] = cur_ref[...]
    for kind in range(3):
        @pl.when(kind_id == kind)
        def _(kind=kind):
            for r in range(0, tm, CONV_ROWS):
                base = CONV_HALO + r - (CONV_WIDTH - 1)
                acc = ext_ref[base:base + CONV_ROWS, :] * w_ref[0:1, :]
                for i in range(1, CONV_WIDTH):
                    acc = acc + ext_ref[base + i:base + i + CONV_ROWS, :] * w_ref[i:i + 1, :]
                o_ref[r:r + CONV_ROWS, :] = _conv_act(acc, kind)


def _conv_prompt(x3, conv_w, key_dim):
    b, t, c = x3.shape
    tm = 528
    assert t % tm == 0 and tm % CONV_ROWS == 0 and tm % CONV_HALO == 0 and key_dim % CONV_LANES == 0
    hb = tm // CONV_HALO
    return pl.pallas_call(
        functools.partial(_conv_prompt_kernel, tm=tm, key_dim=key_dim),
        grid=(b, t // tm, c // CONV_LANES),
        in_specs=[pl.BlockSpec((None, CONV_HALO, CONV_LANES),
                               lambda i, j, l: (i, jnp.maximum(j * hb - 1, 0), l)),
                  pl.BlockSpec((None, tm, CONV_LANES), lambda i, j, l: (i, j, l)),
                  pl.BlockSpec((CONV_WIDTH, CONV_LANES), lambda i, j, l: (0, l))],
        out_specs=pl.BlockSpec((None, tm, CONV_LANES), lambda i, j, l: (i, j, l)),
        out_shape=jax.ShapeDtypeStruct((b, t, c), f32),
        scratch_shapes=[pltpu.VMEM((CONV_HALO + tm, CONV_LANES), f32)],
        compiler_params=_params(("parallel", "parallel", "parallel")),
        name="conv_prompt",
    )(x3, x3, conv_w)


def _conv_sample_kernel(st_ref, x_ref, w_ref, o_ref, *, key_dim):
    kind_id = _conv_kind(pl.program_id(0), key_dim)
    for kind in range(3):
        @pl.when(kind_id == kind)
        def _(kind=kind):
            acc = x_ref[...] * w_ref[CONV_WIDTH - 1:CONV_WIDTH, :]
            for i in range(CONV_WIDTH - 1):
                acc = acc + st_ref[:, i, :] * w_ref[i:i + 1, :]
            o_ref[...] = _conv_act(acc, kind)


def _conv_sample(state, x, conv_w, key_dim):
    b, c = x.shape
    return pl.pallas_call(
        functools.partial(_conv_sample_kernel, key_dim=key_dim),
        grid=(c // CONV_LANES,),
        in_specs=[pl.BlockSpec((b, CONV_WIDTH - 1, CONV_LANES), lambda l: (0, 0, l)),
                  pl.BlockSpec((b, CONV_LANES), lambda l: (0, l)),
                  pl.BlockSpec((CONV_WIDTH, CONV_LANES), lambda l: (0, l))],
        out_specs=pl.BlockSpec((b, CONV_LANES), lambda l: (0, l)),
        out_shape=jax.ShapeDtypeStruct((b, c), f32),
        compiler_params=_params(("parallel",)),
        name="conv_sample",
    )(state, x, conv_w)


def _gate_kernel(ba_ref, a_ref, dt_ref, beta_ref, gcb_ref, gc_ref, *, rows, n_heads, cumulative):
    x = ba_ref[...]
    beta = jax.nn.sigmoid(x)
    xa = x + dt_ref[...]
    softplus = jnp.maximum(xa, 0.0) + jnp.log1p(jnp.exp(-jnp.abs(xa)))
    g = -jnp.exp(a_ref[...]) * softplus
    if cumulative:
        row = lax.broadcasted_iota(jnp.int32, (rows, 1), 0)
        g = jnp.where(row < jnp.where(pl.program_id(1) == 0, FRONT, 0), 0.0, g)
        shift = 1
        while shift < rows:
            g = g + jnp.where(row >= shift, pltpu.roll(g, shift, 0), 0.0)
            shift *= 2
    gc_ref[...] = g
    for h in range(n_heads):
        sl = slice(h * HEAD_DIM, (h + 1) * HEAD_DIM)
        beta_ref[:, sl] = jnp.broadcast_to(beta[:, h:h + 1], (rows, HEAD_DIM))
        gcb_ref[:, sl] = jnp.broadcast_to(g[:, n_heads + h:n_heads + h + 1], (rows, HEAD_DIM))


def _gates(ba3, a_log, dt_bias, cumulative):
    b, t, w = ba3.shape
    n_heads = w // 2
    rows = CHUNK if cumulative else t
    zeros = jnp.zeros((n_heads,), f32)
    a_pad = jnp.concatenate([zeros, a_log.astype(f32)]).reshape(1, w)
    dt_pad = jnp.concatenate([zeros, dt_bias.astype(f32)]).reshape(1, w)
    wide = n_heads * HEAD_DIM
    return pl.pallas_call(
        functools.partial(_gate_kernel, rows=rows, n_heads=n_heads, cumulative=cumulative),
        grid=(b, t // rows),
        in_specs=[pl.BlockSpec((None, rows, w), lambda i, j: (i, j, 0)),
                  pl.BlockSpec((1, w), lambda i, j: (0, 0)),
                  pl.BlockSpec((1, w), lambda i, j: (0, 0))],
        out_specs=[pl.BlockSpec((None, rows, wide), lambda i, j: (i, j, 0)),
                   pl.BlockSpec((None, rows, wide), lambda i, j: (i, j, 0)),
                   pl.BlockSpec((None, rows, w), lambda i, j: (i, j, 0))],
        out_shape=[jax.ShapeDtypeStruct((b, t, wide), f32),
                   jax.ShapeDtypeStruct((b, t, wide), f32),
                   jax.ShapeDtypeStruct((b, t, w), f32)],
        compiler_params=_params(("parallel", "parallel")),
        name="gdn_gates",
    )(ba3, a_pad, dt_pad)


def _dot_nt(a, b):
    return lax.dot_general(a, b, (((1,), (1,)), ((), ())), preferred_element_type=f32)


def _dot_tn(a, b):
    return lax.dot_general(a, b, (((0,), (0,)), ((), ())), preferred_element_type=f32)


def _split(a):
    hi = a.astype(bf16)
    lo = (a - hi.astype(f32)).astype(bf16)
    return hi, lo


def _dot3(a, b):
    a_hi, a_lo = _split(a)
    b_hi, b_lo = _split(b)
    return (jnp.dot(a_hi, b_hi, preferred_element_type=f32)
            + (jnp.dot(a_hi, b_lo, preferred_element_type=f32)
               + jnp.dot(a_lo, b_hi, preferred_element_type=f32)))


def _stack(x, idx):
    return jnp.concatenate([x[:, i * HEAD_DIM:(i + 1) * HEAD_DIM] for i in idx], axis=0)


def _gdn_chunk_kernel(q_ref, k_ref, v_ref, beta_ref, gcb_ref, gcr_ref, z_ref, ng_ref,
                      o_ref, sout_ref, s_ref):
    c = pl.program_id(2)
    rows = QUAD * CHUNK
    heads = tuple(range(QUAD))
    khead = tuple(h // 2 for h in heads)

    @pl.when(c == 0)
    def _():
        s_ref[...] = jnp.zeros_like(s_ref)

    gcb = gcb_ref[...]
    kst = _stack(k_ref[...], khead)
    qst = _stack(q_ref[...], khead)
    vst = _stack(v_ref[...], heads)
    bst = _stack(beta_ref[...], heads)
    gst = _stack(gcb, heads)
    glast = jnp.concatenate(
        [jnp.broadcast_to(gcb[CHUNK - 1:CHUNK, h * HEAD_DIM:(h + 1) * HEAD_DIM], (CHUNK, HEAD_DIM))
         for h in heads], axis=0)
    eg = jnp.exp(gst)
    kb = kst * bst
    rhs = jnp.concatenate([vst * bst, kb * eg], axis=1)
    qd = qst * eg
    kd = kst * jnp.exp(glast - gst)

    ri = lax.broadcasted_iota(jnp.int32, (rows, rows), 0)
    ci = lax.broadcasted_iota(jnp.int32, (rows, rows), 1)
    shift = CHUNK.bit_length() - 1
    same = lax.shift_right_logical(ri, shift) == lax.shift_right_logical(ci, shift)
    diff = jnp.concatenate([gst] * (rows // HEAD_DIM), axis=1) - gcr_ref[...]
    decay = jnp.exp(jnp.where(same & (ri >= ci), diff, -jnp.inf))
    k16 = kst.astype(bf16)
    a_mat = jnp.where(ri > ci, _dot_nt(kb.astype(bf16), k16) * decay, 0.0)
    attn = _dot_nt(qst.astype(bf16), k16) * decay

    pw = -a_mat
    inv = jnp.where(ri == ci, 1.0, 0.0) + pw
    for _ in range(NEUMANN_LEVELS - 1):
        pw = _dot3(pw, pw)
        inv = inv + _dot3(inv, pw)
    sol = _dot3(inv, rhs)
    u = sol[:, :HEAD_DIM]
    w = sol[:, HEAD_DIM:]

    v_new, o_state = [], []
    for h in heads:
        rs = slice(h * CHUNK, (h + 1) * CHUNK)
        lhs = jnp.concatenate([w[rs], qd[rs]], axis=0).astype(bf16)
        x = jnp.dot(lhs, s_ref[h].astype(bf16), preferred_element_type=f32)
        v_new.append(u[rs] - x[:CHUNK])
        o_state.append(x[CHUNK:])
    vn16 = jnp.concatenate(v_new, axis=0).astype(bf16)
    o = jnp.concatenate(o_state, axis=0) + jnp.dot(attn.astype(bf16), vn16,
                                                   preferred_element_type=f32)
    kd16 = kd.astype(bf16)
    for h in heads:
        rs = slice(h * CHUNK, (h + 1) * CHUNK)
        g_last = jnp.exp(gcb[CHUNK - 1:CHUNK, h * HEAD_DIM:(h + 1) * HEAD_DIM])
        s_ref[h] = s_ref[h] * g_last + _dot_tn(kd16[rs], vn16[rs])

    on = o * lax.rsqrt(jnp.mean(o * o, axis=-1, keepdims=True) + EPS) * ng_ref[...]
    z = z_ref[...]
    for h in heads:
        sl = slice(h * HEAD_DIM, (h + 1) * HEAD_DIM)
        o_ref[:, sl] = (on[h * CHUNK:(h + 1) * CHUNK] * _silu(z[:, sl])).astype(o_ref.dtype)

    @pl.when(c == pl.num_programs(2) - 1)
    def _():
        sout_ref[...] = s_ref[...]


def _gdn_chunked(qkv, beta_b, gc_b, gc_rows, z, norm_g, batch):
    m, value_dim = z.shape
    key_dim = (qkv.shape[1] - value_dim) // 2
    n_heads = value_dim // HEAD_DIM
    n_chunks = m // batch // CHUNK
    n_quads = n_heads // QUAD
    kq_w = QUAD // 2 * HEAD_DIM
    v_w = QUAD * HEAD_DIM

    def row_spec(width, col0=0):
        off = col0 // width
        return pl.BlockSpec((CHUNK, width), lambda b, qd, c: (b * n_chunks + c, qd + off))

    return pl.pallas_call(
        _gdn_chunk_kernel,
        grid=(batch, n_quads, n_chunks),
        in_specs=[row_spec(kq_w), row_spec(kq_w, key_dim), row_spec(v_w, 2 * key_dim),
                  row_spec(v_w), row_spec(v_w),
                  pl.BlockSpec((None, None, None, 1, QUAD * CHUNK),
                               lambda b, qd, c: (b, c, qd, 0, 0)),
                  row_spec(v_w),
                  pl.BlockSpec((1, HEAD_DIM), lambda b, qd, c: (0, 0))],
        out_specs=[row_spec(v_w),
                   pl.BlockSpec((None, QUAD, HEAD_DIM, HEAD_DIM), lambda b, qd, c: (b, qd, 0, 0))],
        out_shape=[jax.ShapeDtypeStruct((m, value_dim), bf16),
                   jax.ShapeDtypeStruct((batch, n_heads, HEAD_DIM, HEAD_DIM), f32)],
        scratch_shapes=[pltpu.VMEM((QUAD, HEAD_DIM, HEAD_DIM), f32)],
        compiler_params=_params(("parallel", "parallel", "arbitrary")),
        name="gdn_chunked",
    )(qkv, qkv, qkv, beta_b, gc_b, gc_rows, z, norm_g.reshape(1, HEAD_DIM))


def _gdn_step_kernel(qt_ref, kt_ref, v_ref, beta_ref, g_ref, z_ref, ng_ref, s_ref,
                     o_ref, sout_ref, *, n_heads):
    rep = n_heads // qt_ref.shape[1]
    for h in range(n_heads):
        kh = h // rep
        kcol = jnp.broadcast_to(kt_ref[:, kh:kh + 1], (HEAD_DIM, HEAD_DIM))
        qcol = jnp.broadcast_to(qt_ref[:, kh:kh + 1], (HEAD_DIM, HEAD_DIM))
        s = s_ref[h] * jnp.exp(g_ref[h:h + 1, :])
        kv = jnp.sum(kcol * s, axis=0, keepdims=True)
        delta = (v_ref[h:h + 1, :] - kv) * beta_ref[h:h + 1, :]
        s = s + kcol * delta
        sout_ref[h] = s
        o = jnp.sum(qcol * s, axis=0, keepdims=True)
        on = o * lax.rsqrt(jnp.mean(o * o, axis=-1, keepdims=True) + EPS) * ng_ref[...]
        o_ref[h:h + 1, :] = on * _silu(z_ref[h:h + 1, :])


def _gdn_step(qt, kt, v3, beta3, g3, z3, norm_g, state):
    b, n_heads, _ = v3.shape
    n_k = qt.shape[2]

    def head_spec():
        return pl.BlockSpec((None, n_heads, HEAD_DIM), lambda i: (i, 0, 0))

    def t_spec():
        return pl.BlockSpec((None, HEAD_DIM, n_k), lambda i: (i, 0, 0))

    s_spec = pl.BlockSpec((None, n_heads, HEAD_DIM, HEAD_DIM), lambda i: (i, 0, 0, 0))
    return pl.pallas_call(
        functools.partial(_gdn_step_kernel, n_heads=n_heads),
        grid=(b,),
        in_specs=[t_spec(), t_spec(), head_spec(), head_spec(), head_spec(), head_spec(),
                  pl.BlockSpec((1, HEAD_DIM), lambda i: (0, 0)), s_spec],
        out_specs=[head_spec(), s_spec],
        out_shape=[jax.ShapeDtypeStruct((b, n_heads, HEAD_DIM), f32),
                   jax.ShapeDtypeStruct(state.shape, f32)],
        compiler_params=_params(("parallel",)),
        name="gdn_step",
    )(qt, kt, v3, beta3, g3, z3, norm_g.reshape(1, HEAD_DIM), state)


def _pool_layer(hp, hs, state, w_in, w_grp, scale, w_out, norm_g, batch):
    e = w_grp.shape[0] * w_grp.shape[1]
    w_in16, w_grp16, w_out16 = w_in.astype(bf16), w_grp.astype(bf16), w_out.astype(bf16)

    def mix(h, diff_fn):
        xn = _rmsnorm(h, norm_g, bf16, 256 if h.shape[0] % 256 == 0 else h.shape[0])
        u = _matmul(xn, w_in16, 0, e)
        z = _matmul(xn, w_in16, e, e)
        act = _grouped_matmul_gate(diff_fn(u), w_grp16, scale, z)
        return _matmul(act, w_out16, 0, w_out.shape[1], residual=h), u

    hp, u_p = mix(hp, lambda u: _pool_diff_prompt(u.reshape(batch, -1, e)).reshape(u.shape))
    hs, u_s = mix(hs, lambda u: _pool_diff_sample(state, u))
    pool_p = u_p.reshape(batch, -1, e)[:, -POOL_CTX:]
    pool_s = jnp.concatenate([state[:, 1:], u_s[:, None]], axis=1)
    return hp, hs, pool_p, pool_s


def _gdn_layer(hp, hs, conv_state, ssm_state, w_in, conv_w, a_log, dt_bias, head_g, w_out,
               norm_g, batch):
    value_dim = w_out.shape[0]
    n_heads = value_dim // HEAD_DIM
    conv_dim = conv_w.shape[1]
    key_dim = (conv_dim - value_dim) // 2
    n_k = key_dim // HEAD_DIM
    w_in16, w_out16 = w_in.astype(bf16), w_out.astype(bf16)

    def project(h):
        xn = _rmsnorm(h, norm_g, bf16, 256 if h.shape[0] % 256 == 0 else h.shape[0])
        qkv = _matmul(xn, w_in16, 0, conv_dim)
        z = _matmul(xn, w_in16, conv_dim, value_dim)
        ba = _matmul(xn, w_in16, conv_dim + value_dim, 2 * n_heads)
        return qkv, z, ba

    qkv, z, ba = project(hp)
    t = hp.shape[0] // batch
    qkv3 = qkv.reshape(batch, t, conv_dim)
    act = _conv_prompt(qkv3, conv_w, key_dim).reshape(-1, conv_dim)
    beta_b, gc_b, gc = _gates(ba.reshape(batch, t, 2 * n_heads), a_log, dt_bias, True)
    gc_rows = gc[..., n_heads:].reshape(batch, t // CHUNK, CHUNK, n_heads // QUAD, QUAD)
    gc_rows = jnp.transpose(gc_rows, (0, 1, 3, 4, 2)).reshape(
        batch, t // CHUNK, n_heads // QUAD, 1, QUAD * CHUNK)
    o, ssm_p = _gdn_chunked(act, beta_b.reshape(-1, value_dim), gc_b.reshape(-1, value_dim),
                            gc_rows, z, head_g, batch)
    hp = _matmul(o, w_out16, 0, w_out.shape[1], residual=hp)
    conv_p = qkv3[:, -(CONV_WIDTH - 1):]

    qkv, z, ba = project(hs)
    bs = hs.shape[0]
    act = _conv_sample(conv_state, qkv, conv_w, key_dim)
    beta_b, g_b, _ = _gates(ba.reshape(1, bs, 2 * n_heads), a_log, dt_bias, False)
    qt = jnp.transpose(act[:, :key_dim].reshape(bs, n_k, HEAD_DIM), (0, 2, 1))
    kt = jnp.transpose(act[:, key_dim:2 * key_dim].reshape(bs, n_k, HEAD_DIM), (0, 2, 1))
    per_head = (bs, n_heads, HEAD_DIM)
    o, ssm_s = _gdn_step(qt, kt, act[:, 2 * key_dim:].reshape(per_head), beta_b.reshape(per_head),
                         g_b.reshape(per_head), z.reshape(per_head), head_g, ssm_state)
    hs = _matmul(o.reshape(bs, value_dim).astype(bf16), w_out16, 0, w_out.shape[1], residual=hs)
    conv_s = jnp.concatenate([conv_state[:, 1:], qkv[:, None]], axis=1)
    return hp, hs, conv_p, ssm_p, conv_s, ssm_s


def kernel(x_prompt, x_sample, state_pool, state_conv, state_ssm, meta_tokens, norm_g, final_norm_g, pool_w_in, pool_w_grp, pool_scale, pool_w_out, gdn_w_in, gdn_conv_w, gdn_A_log, gdn_dt_bias, gdn_norm_g, gdn_w_out):
    dt = x_prompt.dtype
    batch, seq, d = x_prompt.shape
    depth = norm_g.shape[0]
    meta = jnp.broadcast_to(meta_tokens.astype(dt)[None], (batch, N_META, d))
    hp = jnp.concatenate([jnp.zeros((batch, FRONT, d), dt), meta, x_prompt], axis=1)
    t = hp.shape[1]
    hp = hp.reshape(batch * t, d)
    hs = x_sample.reshape(x_sample.shape[0], d)
    pool_p, conv_p, ssm_p, pool_s, conv_s, ssm_s = [], [], [], [], [], []
    for i in range(depth):
        j = i // 2
        if i % 2 == 0:
            hp, hs, pp, ps = _pool_layer(hp, hs, state_pool[j], pool_w_in[j], pool_w_grp[j],
                                         pool_scale[j], pool_w_out[j], norm_g[i], batch)
            pool_p.append(pp)
            pool_s.append(ps)
        else:
            hp, hs, cp, sp, cs, ss = _gdn_layer(hp, hs, state_conv[j], state_ssm[j], gdn_w_in[j],
                                                gdn_conv_w[j], gdn_A_log[j], gdn_dt_bias[j],
                                                gdn_norm_g[j], gdn_w_out[j], norm_g[i], batch)
            conv_p.append(cp)
            ssm_p.append(sp)
            conv_s.append(cs)
            ssm_s.append(ss)
    y_prompt = _final_norm_prompt(hp.reshape(batch, t, d), final_norm_g, seq)
    y_sample = _rmsnorm(hs, final_norm_g, dt, hs.shape[0]).reshape(x_sample.shape)
    return (y_prompt, y_sample, jnp.stack(pool_p), jnp.stack(conv_p), jnp.stack(ssm_p),
            jnp.stack(pool_s), jnp.stack(conv_s), jnp.stack(ssm_s))
```

```python
import functools

import jax
import jax.numpy as jnp
from jax import lax
from jax.experimental import pallas as pl
from jax.experimental.pallas import tpu as pltpu

N_META = 16
PAST_LEN = 16384
POOL_WINDOWS = (2, 4, 8, 16)
POOL_CTX = max(POOL_WINDOWS) - 1
HEAD_DIM = 128
CONV_WIDTH = 4
CHUNK = 64
EPS = 1e-6
FRONT = (-N_META) % CHUNK
QUAD = 4
QUADS_PER_STEP = 4
NEUMANN_LEVELS = 6

VMEM_LIMIT = 56 * 1024 * 1024

f32 = jnp.float32
bf16 = jnp.bfloat16


def _params(sem):
    return pltpu.CompilerParams(dimension_semantics=sem, vmem_limit_bytes=VMEM_LIMIT)


def _silu(x):
    return x * jax.nn.sigmoid(x)


def _stacked_out(stack, layer, shape, dtype):
    if stack is None:
        return jax.ShapeDtypeStruct(shape, dtype), [], [], lambda n_in, n_out: {}
    return (jax.ShapeDtypeStruct(stack.shape, stack.dtype), [stack],
            [pl.BlockSpec(memory_space=pl.ANY)], lambda n_in, n_out: {n_in: n_out})


def _rmsnorm_kernel(x_ref, g_ref, o_ref):
    x = x_ref[...]
    y = x * lax.rsqrt(jnp.mean(x * x, axis=-1, keepdims=True) + EPS)
    o_ref[...] = (y * g_ref[...]).astype(o_ref.dtype)


def _rmsnorm(x, g, out_dtype):
    m, d = x.shape
    tm = 256 if m % 256 == 0 else m
    return pl.pallas_call(
        _rmsnorm_kernel,
        grid=(m // tm,),
        in_specs=[pl.BlockSpec((tm, d), lambda i: (i, 0)),
                  pl.BlockSpec((1, d), lambda i: (0, 0))],
        out_specs=pl.BlockSpec((tm, d), lambda i: (i, 0)),
        out_shape=jax.ShapeDtypeStruct((m, d), out_dtype),
        compiler_params=_params(("parallel",)),
        name="rmsnorm",
    )(x, g.reshape(1, d))


def _final_norm_prompt(h3, g, seq):
    b, t, d = h3.shape
    skip = (t - seq) // CHUNK
    return pl.pallas_call(
        _rmsnorm_kernel,
        grid=(b, seq // CHUNK),
        in_specs=[pl.BlockSpec((None, CHUNK, d), lambda i, j: (i, j + skip, 0)),
                  pl.BlockSpec((1, d), lambda i, j: (0, 0))],
        out_specs=pl.BlockSpec((None, CHUNK, d), lambda i, j: (i, j, 0)),
        out_shape=jax.ShapeDtypeStruct((b, seq, d), h3.dtype),
        compiler_params=_params(("parallel", "parallel")),
        name="final_norm",
    )(h3, g.reshape(1, d))


def _mm_kernel(*refs, epilogue):
    x_ref, w_ref = refs[0], refs[1]
    o_ref = refs[-1]
    acc = jnp.dot(x_ref[...], w_ref[...], preferred_element_type=f32)
    if epilogue == "plain":
        o_ref[...] = acc.astype(o_ref.dtype)
    elif epilogue == "gate":
        scale_ref, z_ref = refs[2], refs[3]
        o_ref[...] = (acc * scale_ref[...] * _silu(z_ref[...])).astype(o_ref.dtype)
    elif epilogue == "residual":
        h_ref = refs[2]
        o_ref[...] = h_ref[...] + acc
    else:
        raise ValueError(epilogue)


def _mm_tiles(m):
    tm = 768 if m % 768 == 0 else m
    return tm, 512


def _matmul(x, w, col0, ncols, out_dtype=f32, residual=None):
    m, k = x.shape
    tm, tn = _mm_tiles(m)
    tn = min(tn, ncols)
    off = col0 // tn
    assert col0 % tn == 0 and ncols % tn == 0 and m % tm == 0
    in_specs = [pl.BlockSpec((tm, k), lambda i, j: (i, 0)),
                pl.BlockSpec((k, tn), lambda i, j: (0, j + off))]
    args = [x, w]
    if residual is not None:
        in_specs.append(pl.BlockSpec((tm, tn), lambda i, j: (i, j)))
        args.append(residual)
    return pl.pallas_call(
        functools.partial(_mm_kernel, epilogue="plain" if residual is None else "residual"),
        grid=(m // tm, ncols // tn),
        in_specs=in_specs,
        out_specs=pl.BlockSpec((tm, tn), lambda i, j: (i, j)),
        out_shape=jax.ShapeDtypeStruct((m, ncols), out_dtype),
        compiler_params=_params(("parallel", "arbitrary")),
        name="matmul",
    )(*args)


def _grouped_matmul_gate(d, w_grp, scale, z):
    m, e = d.shape
    ng, gk, gn = w_grp.shape
    tm, tn = _mm_tiles(m)
    nb = gn // tn
    return pl.pallas_call(
        functools.partial(_mm_kernel, epilogue="gate"),
        grid=(m // tm, ng, nb),
        in_specs=[pl.BlockSpec((tm, gk), lambda i, g, j: (i, g)),
                  pl.BlockSpec((None, gk, tn), lambda i, g, j: (g, 0, j)),
                  pl.BlockSpec((1, tn), lambda i, g, j: (0, g * nb + j)),
                  pl.BlockSpec((tm, tn), lambda i, g, j: (i, g * nb + j))],
        out_specs=pl.BlockSpec((tm, tn), lambda i, g, j: (i, g * nb + j)),
        out_shape=jax.ShapeDtypeStruct((m, e), bf16),
        compiler_params=_params(("parallel", "arbitrary", "arbitrary")),
        name="grouped_matmul_gate",
    )(d, w_grp, scale.reshape(1, e), z)


POOL_HALO = 16
POOL_ROWS = 16
POOL_LANES = 512


def _pool_diff_prompt_kernel(halo_ref, cur_ref, o_ref, ext_ref, *, tm, gsz):
    t = pl.program_id(1)
    gi = (pl.program_id(2) * POOL_LANES) // gsz
    ext_ref[0:POOL_HALO, :] = jnp.where(t > 0, halo_ref[...], 0.0)
    ext_ref[POOL_HALO:POOL_HALO + tm, :] = cur_ref[...]
    for widx, w in enumerate(POOL_WINDOWS):
        @pl.when(gi == widx)
        def _(w=w):
            for r in range(0, tm, POOL_ROWS):
                base = POOL_HALO + r
                cur = ext_ref[base:base + POOL_ROWS, :]
                s = cur
                for i in range(1, w):
                    s = s + ext_ref[base - i:base - i + POOL_ROWS, :]
                pos = lax.broadcasted_iota(jnp.int32, (POOL_ROWS, 1), 0) + (t * tm + r - FRONT)
                cnt = jnp.clip(pos + 1, 1, w).astype(f32)
                o_ref[r:r + POOL_ROWS, :] = (s / cnt - cur).astype(o_ref.dtype)


def _pool_diff_prompt(u3):
    b, t, e = u3.shape
    tm = 528
    assert t % tm == 0 and tm % POOL_HALO == 0 and e % (len(POOL_WINDOWS) * POOL_LANES) == 0
    hb = tm // POOL_HALO
    return pl.pallas_call(
        functools.partial(_pool_diff_prompt_kernel, tm=tm, gsz=e // len(POOL_WINDOWS)),
        grid=(b, t // tm, e // POOL_LANES),
        in_specs=[pl.BlockSpec((None, POOL_HALO, POOL_LANES),
                               lambda i, j, c: (i, jnp.maximum(j * hb - 1, 0), c)),
                  pl.BlockSpec((None, tm, POOL_LANES), lambda i, j, c: (i, j, c))],
        out_specs=pl.BlockSpec((None, tm, POOL_LANES), lambda i, j, c: (i, j, c)),
        out_shape=jax.ShapeDtypeStruct((b, t, e), bf16),
        scratch_shapes=[pltpu.VMEM((POOL_HALO + tm, POOL_LANES), f32)],
        compiler_params=_params(("parallel", "parallel", "parallel")),
        name="pool_diff_prompt",
    )(u3, u3)


def _pool_diff_sample_kernel(*refs, gsz, lanes):
    st_ref, u_ref = refs[0], refs[1]
    o_ref, ns_ref = refs[-2], refs[-1]
    gi = (pl.program_id(1) * lanes) // gsz
    for widx, w in enumerate(POOL_WINDOWS):
        @pl.when(gi == widx)
        def _(w=w):
            cur = u_ref[...]
            s = cur
            for i in range(1, w):
                s = s + st_ref[:, POOL_CTX - i, :]
            cnt = float(min(PAST_LEN + 1, w))
            o_ref[...] = (s / cnt - cur).astype(o_ref.dtype)
    for r in range(POOL_CTX - 1):
        ns_ref[:, r, :] = st_ref[:, r + 1, :]
    ns_ref[:, POOL_CTX - 1, :] = u_ref[...]


def _pool_diff_sample(states, layer, u, new_states):
    _, b, _, e = states.shape
    gsz = e // len(POOL_WINDOWS)
    bb, lanes = 16, min(1024, gsz)
    assert b % bb == 0 and gsz % lanes == 0
    st_spec = pl.BlockSpec((None, bb, POOL_CTX, lanes), lambda i, c: (layer, i, 0, c))
    ns_shape, extra, extra_specs, alias = _stacked_out(new_states, layer, states.shape, states.dtype)
    return pl.pallas_call(
        functools.partial(_pool_diff_sample_kernel, gsz=gsz, lanes=lanes),
        grid=(b // bb, e // lanes),
        in_specs=[st_spec, pl.BlockSpec((bb, lanes), lambda i, c: (i, c))] + extra_specs,
        out_specs=[pl.BlockSpec((bb, lanes), lambda i, c: (i, c)), st_spec],
        out_shape=[jax.ShapeDtypeStruct((b, e), bf16), ns_shape],
        input_output_aliases=alias(2, 1),
        compiler_params=_params(("parallel", "parallel")),
        name="pool_diff_sample",
    )(states, u, *extra)


CONV_HALO = 8
CONV_ROWS = 24
CONV_LANES = 1024


def _conv_act(acc, kind):
    y = _silu(acc)
    if kind == 2:
        return y
    parts = []
    for h in range(y.shape[1] // HEAD_DIM):
        yh = y[:, h * HEAD_DIM:(h + 1) * HEAD_DIM]
        yh = yh * lax.rsqrt(jnp.sum(yh * yh, axis=-1, keepdims=True) + EPS)
        if kind == 0:
            yh = yh * (HEAD_DIM ** -0.5)
        parts.append(yh)
    return jnp.concatenate(parts, axis=1)


def _conv_kind(c, key_dim):
    return jnp.minimum((c * CONV_LANES) // key_dim, 2)


def _conv_prompt_kernel(halo_ref, cur_ref, w_ref, o_ref, ext_ref, *, tm, key_dim):
    t = pl.program_id(1)
    kind_id = _conv_kind(pl.program_id(2), key_dim)
    ext_ref[0:CONV_HALO, :] = jnp.where(t > 0, halo_ref[...], 0.0)
    ext_ref[CONV_HALO:CONV_HALO + tm, :] = cur_ref[...]
    for kind in range(3):
        @pl.when(kind_id == kind)
        def _(kind=kind):
            for r in range(0, tm, CONV_ROWS):
                base = CONV_HALO + r - (CONV_WIDTH - 1)
                acc = ext_ref[base:base + CONV_ROWS, :] * w_ref[0:1, :]
                for i in range(1, CONV_WIDTH):
                    acc = acc + ext_ref[base + i:base + i + CONV_ROWS, :] * w_ref[i:i + 1, :]
                o_ref[r:r + CONV_ROWS, :] = _conv_act(acc, kind)


def _conv_prompt(x3, conv_w, key_dim):
    b, t, c = x3.shape
    tm = 528
    assert t % tm == 0 and tm % CONV_ROWS == 0 and tm % CONV_HALO == 0 and key_dim % CONV_LANES == 0
    hb = tm // CONV_HALO
    return pl.pallas_call(
        functools.partial(_conv_prompt_kernel, tm=tm, key_dim=key_dim),
        grid=(b, t // tm, c // CONV_LANES),
        in_specs=[pl.BlockSpec((None, CONV_HALO, CONV_LANES),
                               lambda i, j, l: (i, jnp.maximum(j * hb - 1, 0), l)),
                  pl.BlockSpec((None, tm, CONV_LANES), lambda i, j, l: (i, j, l)),
                  pl.BlockSpec((CONV_WIDTH, CONV_LANES), lambda i, j, l: (0, l))],
        out_specs=pl.BlockSpec((None, tm, CONV_LANES), lambda i, j, l: (i, j, l)),
        out_shape=jax.ShapeDtypeStruct((b, t, c), f32),
        scratch_shapes=[pltpu.VMEM((CONV_HALO + tm, CONV_LANES), f32)],
        compiler_params=_params(("parallel", "parallel", "parallel")),
        name="conv_prompt",
    )(x3, x3, conv_w)


def _conv_sample_kernel(*refs, key_dim):
    st_ref, x_ref, w_ref = refs[0], refs[1], refs[2]
    o_ref, ns_ref = refs[-2], refs[-1]
    kind_id = _conv_kind(pl.program_id(0), key_dim)
    for kind in range(3):
        @pl.when(kind_id == kind)
        def _(kind=kind):
            acc = x_ref[...] * w_ref[CONV_WIDTH - 1:CONV_WIDTH, :]
            for i in range(CONV_WIDTH - 1):
                acc = acc + st_ref[:, i, :] * w_ref[i:i + 1, :]
            o_ref[...] = _conv_act(acc, kind)
    for r in range(CONV_WIDTH - 2):
        ns_ref[:, r, :] = st_ref[:, r + 1, :]
    ns_ref[:, CONV_WIDTH - 2, :] = x_ref[...]


def _conv_sample(states, layer, x, conv_w, key_dim, new_states):
    b, c = x.shape
    st_spec = pl.BlockSpec((None, b, CONV_WIDTH - 1, CONV_LANES), lambda l: (layer, 0, 0, l))
    ns_shape, extra, extra_specs, alias = _stacked_out(new_states, layer, states.shape, states.dtype)
    return pl.pallas_call(
        functools.partial(_conv_sample_kernel, key_dim=key_dim),
        grid=(c // CONV_LANES,),
        in_specs=[st_spec,
                  pl.BlockSpec((b, CONV_LANES), lambda l: (0, l)),
                  pl.BlockSpec((CONV_WIDTH, CONV_LANES), lambda l: (0, l))] + extra_specs,
        out_specs=[pl.BlockSpec((b, CONV_LANES), lambda l: (0, l)), st_spec],
        out_shape=[jax.ShapeDtypeStruct((b, c), f32), ns_shape],
        input_output_aliases=alias(3, 1),
        compiler_params=_params(("parallel",)),
        name="conv_sample",
    )(states, x, conv_w, *extra)


def _gate_kernel(ba_ref, a_ref, dt_ref, beta_ref, gcb_ref, gc_ref, *, rows, n_heads, cumulative):
    x = ba_ref[...]
    beta = jax.nn.sigmoid(x)
    xa = x + dt_ref[...]
    softplus = jnp.maximum(xa, 0.0) + jnp.log1p(jnp.exp(-jnp.abs(xa)))
    g = -jnp.exp(a_ref[...]) * softplus
    if cumulative:
        row = lax.broadcasted_iota(jnp.int32, (rows, 1), 0)
        g = jnp.where(row < jnp.where(pl.program_id(1) == 0, FRONT, 0), 0.0, g)
        shift = 1
        while shift < rows:
            g = g + jnp.where(row >= shift, pltpu.roll(g, shift, 0), 0.0)
            shift *= 2
    gc_ref[...] = g
    for h in range(n_heads):
        sl = slice(h * HEAD_DIM, (h + 1) * HEAD_DIM)
        beta_ref[:, sl] = jnp.broadcast_to(beta[:, h:h + 1], (rows, HEAD_DIM))
        gcb_ref[:, sl] = jnp.broadcast_to(g[:, n_heads + h:n_heads + h + 1], (rows, HEAD_DIM))


def _gates(ba3, a_log, dt_bias, cumulative):
    b, t, w = ba3.shape
    n_heads = w // 2
    rows = CHUNK if cumulative else t
    zeros = jnp.zeros((n_heads,), f32)
    a_pad = jnp.concatenate([zeros, a_log.astype(f32)]).reshape(1, w)
    dt_pad = jnp.concatenate([zeros, dt_bias.astype(f32)]).reshape(1, w)
    wide = n_heads * HEAD_DIM
    return pl.pallas_call(
        functools.partial(_gate_kernel, rows=rows, n_heads=n_heads, cumulative=cumulative),
        grid=(b, t // rows),
        in_specs=[pl.BlockSpec((None, rows, w), lambda i, j: (i, j, 0)),
                  pl.BlockSpec((1, w), lambda i, j: (0, 0)),
                  pl.BlockSpec((1, w), lambda i, j: (0, 0))],
        out_specs=[pl.BlockSpec((None, rows, wide), lambda i, j: (i, j, 0)),
                   pl.BlockSpec((None, rows, wide), lambda i, j: (i, j, 0)),
                   pl.BlockSpec((None, rows, w), lambda i, j: (i, j, 0))],
        out_shape=[jax.ShapeDtypeStruct((b, t, wide), f32),
                   jax.ShapeDtypeStruct((b, t, wide), f32),
                   jax.ShapeDtypeStruct((b, t, w), f32)],
        compiler_params=_params(("parallel", "parallel")),
        name="gdn_gates",
    )(ba3, a_pad, dt_pad)


def _dot_nt(a, b):
    return lax.dot_general(a, b, (((1,), (1,)), ((), ())), preferred_element_type=f32)


def _dot_tn(a, b):
    return lax.dot_general(a, b, (((0,), (0,)), ((), ())), preferred_element_type=f32)


def _dot16(a, b):
    return jnp.dot(a.astype(bf16), b.astype(bf16), preferred_element_type=f32)


def _dot_split(a, b):
    a_hi = a.astype(bf16)
    a_lo = (a - a_hi.astype(f32)).astype(bf16)
    b_hi = b.astype(bf16)
    b_lo = (b - b_hi.astype(f32)).astype(bf16)
    return (jnp.dot(a_hi, b_hi, preferred_element_type=f32)
            + (jnp.dot(a_hi, b_lo, preferred_element_type=f32)
               + jnp.dot(a_lo, b_hi, preferred_element_type=f32)))


def _stack(x, idx):
    return jnp.concatenate([x[:, i * HEAD_DIM:(i + 1) * HEAD_DIM] for i in idx], axis=0)


def _gdn_chunk_kernel(q_ref, k_ref, v_ref, beta_ref, gcb_ref, gcr_ref, z_ref, ng_ref,
                      o_ref, sout_ref, s_ref):
    c = pl.program_id(2)

    @pl.when(c == 0)
    def _():
        s_ref[...] = jnp.zeros_like(s_ref)

    quads = range(QUADS_PER_STEP)
    heads = tuple(range(QUAD))
    khead = tuple(h // 2 for h in heads)
    rows = QUAD * CHUNK
    kq_w = QUAD // 2 * HEAD_DIM
    v_w = QUAD * HEAD_DIM

    def head_rows(h):
        return slice(h * CHUNK, (h + 1) * CHUNK)

    def head_lanes(h):
        return slice(h * HEAD_DIM, (h + 1) * HEAD_DIM)

    ri = lax.broadcasted_iota(jnp.int32, (rows, rows), 0)
    ci = lax.broadcasted_iota(jnp.int32, (rows, rows), 1)
    shift = CHUNK.bit_length() - 1
    same = lax.shift_right_logical(ri, shift) == lax.shift_right_logical(ci, shift)
    lower = same & (ri >= ci)
    strict = ri > ci

    k16, kb16, q16, rhs, qd, kd16, decay, g_last = [], [], [], [], [], [], [], []
    for i in quads:
        kq = slice(i * kq_w, (i + 1) * kq_w)
        vs = slice(i * v_w, (i + 1) * v_w)
        gcb = gcb_ref[:, vs]
        kst = _stack(k_ref[:, kq], khead)
        qst = _stack(q_ref[:, kq], khead)
        bst = _stack(beta_ref[:, vs], heads)
        gst = _stack(gcb, heads)
        last = [gcb[CHUNK - 1:CHUNK, head_lanes(h)] for h in heads]
        glast = jnp.concatenate([jnp.broadcast_to(r, (CHUNK, HEAD_DIM)) for r in last], axis=0)
        eg = jnp.exp(gst)
        kb = kst * bst
        k16.append(kst.astype(bf16))
        kb16.append(kb.astype(bf16))
        q16.append(qst.astype(bf16))
        rhs.append(jnp.concatenate([_stack(v_ref[:, vs], heads) * bst, kb * eg], axis=1))
        qd.append(qst * eg)
        kd16.append((kst * jnp.exp(glast - gst)).astype(bf16))
        diff = jnp.concatenate([gst] * (rows // HEAD_DIM), axis=1) - gcr_ref[i]
        decay.append(jnp.exp(jnp.where(lower, diff, -jnp.inf)))
        g_last.append([jnp.exp(r) for r in last])

    pw = [-jnp.where(strict, _dot_nt(kb16[i], k16[i]) * decay[i], 0.0) for i in quads]
    attn16 = [(_dot_nt(q16[i], k16[i]) * decay[i]).astype(bf16) for i in quads]

    sol = [rhs[i] + _dot_split(pw[i], rhs[i]) for i in quads]
    for _ in range(NEUMANN_LEVELS - 1):
        pw16 = [p.astype(bf16) for p in pw]
        pw = [jnp.dot(p, p, preferred_element_type=f32) for p in pw16]
        sol = [sol[i] + _dot16(pw[i], sol[i]) for i in quads]

    x = [[_dot16(jnp.concatenate([sol[i][head_rows(h), HEAD_DIM:], qd[i][head_rows(h)]], axis=0),
                 s_ref[i * QUAD + h]) for h in heads] for i in quads]
    vn16 = [jnp.concatenate([sol[i][head_rows(h), :HEAD_DIM] - x[i][h][:CHUNK] for h in heads],
                            axis=0).astype(bf16) for i in quads]
    o = [jnp.concatenate([x[i][h][CHUNK:] for h in heads], axis=0)
         + jnp.dot(attn16[i], vn16[i], preferred_element_type=f32) for i in quads]
    for i in quads:
        for h in heads:
            s_ref[i * QUAD + h] = (s_ref[i * QUAD + h] * g_last[i][h]
                                   + _dot_tn(kd16[i][head_rows(h)], vn16[i][head_rows(h)]))

    for i in quads:
        on = o[i] * lax.rsqrt(jnp.mean(o[i] * o[i], axis=-1, keepdims=True) + EPS) * ng_ref[...]
        for h in heads:
            sl = slice(i * v_w + h * HEAD_DIM, i * v_w + (h + 1) * HEAD_DIM)
            o_ref[:, sl] = (on[head_rows(h)] * _silu(z_ref[:, sl])).astype(o_ref.dtype)

    @pl.when(c == pl.num_programs(2) - 1)
    def _():
        sout_ref[...] = s_ref[...]


def _gdn_chunked(qkv, beta_b, gc_b, gc_rows, z, norm_g, batch):
    m, value_dim = z.shape
    key_dim = (qkv.shape[1] - value_dim) // 2
    n_heads = value_dim // HEAD_DIM
    n_chunks = m // batch // CHUNK
    heads_per_step = QUAD * QUADS_PER_STEP
    n_groups = n_heads // heads_per_step
    kq_w = heads_per_step // 2 * HEAD_DIM
    v_w = heads_per_step * HEAD_DIM

    def row_spec(width, col0=0):
        off = col0 // width
        return pl.BlockSpec((CHUNK, width), lambda b, g, c: (b * n_chunks + c, g + off))

    return pl.pallas_call(
        _gdn_chunk_kernel,
        grid=(batch, n_groups, n_chunks),
        in_specs=[row_spec(kq_w), row_spec(kq_w, key_dim), row_spec(v_w, 2 * key_dim),
                  row_spec(v_w), row_spec(v_w),
                  pl.BlockSpec((None, None, QUADS_PER_STEP, 1, QUAD * CHUNK),
                               lambda b, g, c: (b, c, g, 0, 0)),
                  row_spec(v_w),
                  pl.BlockSpec((1, HEAD_DIM), lambda b, g, c: (0, 0))],
        out_specs=[row_spec(v_w),
                   pl.BlockSpec((None, heads_per_step, HEAD_DIM, HEAD_DIM),
                                lambda b, g, c: (b, g, 0, 0))],
        out_shape=[jax.ShapeDtypeStruct((m, value_dim), bf16),
                   jax.ShapeDtypeStruct((batch, n_heads, HEAD_DIM, HEAD_DIM), f32)],
        scratch_shapes=[pltpu.VMEM((heads_per_step, HEAD_DIM, HEAD_DIM), f32)],
        compiler_params=_params(("parallel", "parallel", "arbitrary")),
        name="gdn_chunked",
    )(qkv, qkv, qkv, beta_b, gc_b, gc_rows, z, norm_g.reshape(1, HEAD_DIM))


def _gdn_step_kernel(*refs, n_heads):
    qt_ref, kt_ref, v_ref, beta_ref, g_ref, z_ref, ng_ref, s_ref = refs[:8]
    o_ref, sout_ref, kcol_ref, qcol_ref, eg_ref, oraw_ref = refs[-6:]
    n_k = qt_ref.shape[1]
    rep = n_heads // n_k
    for kh in range(n_k):
        kcol_ref[kh] = jnp.broadcast_to(kt_ref[:, kh:kh + 1], (HEAD_DIM, HEAD_DIM))
        qcol_ref[kh] = jnp.broadcast_to(qt_ref[:, kh:kh + 1], (HEAD_DIM, HEAD_DIM))
    eg_ref[...] = jnp.exp(g_ref[...])
    for h in range(n_heads):
        kcol = kcol_ref[h // rep]
        s = s_ref[h] * eg_ref[h:h + 1, :]
        kv = jnp.sum(kcol * s, axis=0, keepdims=True)
        delta = (v_ref[h:h + 1, :] - kv) * beta_ref[h:h + 1, :]
        s = s + kcol * delta
        sout_ref[h] = s
        oraw_ref[h:h + 1, :] = jnp.sum(qcol_ref[h // rep] * s, axis=0, keepdims=True)
    o = oraw_ref[...]
    on = o * lax.rsqrt(jnp.mean(o * o, axis=-1, keepdims=True) + EPS) * ng_ref[...]
    o_ref[...] = on * _silu(z_ref[...])


def _gdn_step(qt, kt, v3, beta3, g3, z3, norm_g, states, layer, new_states):
    b, n_heads, _ = v3.shape
    n_k = qt.shape[2]

    def head_spec():
        return pl.BlockSpec((None, n_heads, HEAD_DIM), lambda i: (i, 0, 0))

    def t_spec():
        return pl.BlockSpec((None, HEAD_DIM, n_k), lambda i: (i, 0, 0))

    s_spec = pl.BlockSpec((None, None, n_heads, HEAD_DIM, HEAD_DIM),
                          lambda i: (layer, i, 0, 0, 0))
    ns_shape, extra, extra_specs, alias = _stacked_out(new_states, layer, states.shape, f32)
    return pl.pallas_call(
        functools.partial(_gdn_step_kernel, n_heads=n_heads),
        grid=(b,),
        in_specs=[t_spec(), t_spec(), head_spec(), head_spec(), head_spec(), head_spec(),
                  pl.BlockSpec((1, HEAD_DIM), lambda i: (0, 0)), s_spec] + extra_specs,
        out_specs=[head_spec(), s_spec],
        out_shape=[jax.ShapeDtypeStruct((b, n_heads, HEAD_DIM), f32), ns_shape],
        scratch_shapes=[pltpu.VMEM((n_k, HEAD_DIM, HEAD_DIM), f32),
                        pltpu.VMEM((n_k, HEAD_DIM, HEAD_DIM), f32),
                        pltpu.VMEM((n_heads, HEAD_DIM), f32),
                        pltpu.VMEM((n_heads, HEAD_DIM), f32)],
        input_output_aliases=alias(8, 1),
        compiler_params=_params(("parallel",)),
        name="gdn_step",
    )(qt, kt, v3, beta3, g3, z3, norm_g.reshape(1, HEAD_DIM), states, *extra)


def _pool_layer(hp, hs, states, layer, new_states, w_in, w_grp, scale, w_out, norm_g, batch):
    e = w_grp.shape[0] * w_grp.shape[1]
    w_in16, w_grp16, w_out16 = w_in.astype(bf16), w_grp.astype(bf16), w_out.astype(bf16)

    def project(h):
        xn = _rmsnorm(h, norm_g, bf16)
        return _matmul(xn, w_in16, 0, e), _matmul(xn, w_in16, e, e)

    def mix(h, d, z):
        act = _grouped_matmul_gate(d, w_grp16, scale, z)
        return _matmul(act, w_out16, 0, w_out.shape[1], residual=h)

    u, z = project(hp)
    u3 = u.reshape(batch, -1, e)
    hp = mix(hp, _pool_diff_prompt(u3).reshape(u.shape), z)
    pool_p = u3[:, -POOL_CTX:]

    u, z = project(hs)
    d, new_states = _pool_diff_sample(states, layer, u, new_states)
    hs = mix(hs, d, z)
    return hp, hs, pool_p, new_states


def _gdn_layer(hp, hs, conv_states, ssm_states, layer, new_conv, new_ssm, w_in, conv_w, a_log,
               dt_bias, head_g, w_out, norm_g, batch):
    value_dim = w_out.shape[0]
    n_heads = value_dim // HEAD_DIM
    conv_dim = conv_w.shape[1]
    key_dim = (conv_dim - value_dim) // 2
    n_k = key_dim // HEAD_DIM
    w_in16, w_out16 = w_in.astype(bf16), w_out.astype(bf16)

    def project(h):
        xn = _rmsnorm(h, norm_g, bf16)
        qkv = _matmul(xn, w_in16, 0, conv_dim)
        z = _matmul(xn, w_in16, conv_dim, value_dim)
        ba = _matmul(xn, w_in16, conv_dim + value_dim, 2 * n_heads)
        return qkv, z, ba

    qkv, z, ba = project(hp)
    t = hp.shape[0] // batch
    qkv3 = qkv.reshape(batch, t, conv_dim)
    act = _conv_prompt(qkv3, conv_w, key_dim).reshape(-1, conv_dim)
    beta_b, gc_b, gc = _gates(ba.reshape(batch, t, 2 * n_heads), a_log, dt_bias, True)
    gc_rows = gc[..., n_heads:].reshape(batch, t // CHUNK, CHUNK, n_heads // QUAD, QUAD)
    gc_rows = jnp.transpose(gc_rows, (0, 1, 3, 4, 2)).reshape(
        batch, t // CHUNK, n_heads // QUAD, 1, QUAD * CHUNK)
    o, ssm_p = _gdn_chunked(act, beta_b.reshape(-1, value_dim), gc_b.reshape(-1, value_dim),
                            gc_rows, z, head_g, batch)
    hp = _matmul(o, w_out16, 0, w_out.shape[1], residual=hp)
    conv_p = qkv3[:, -(CONV_WIDTH - 1):]

    qkv, z, ba = project(hs)
    bs = hs.shape[0]
    act, new_conv = _conv_sample(conv_states, layer, qkv, conv_w, key_dim, new_conv)
    beta_b, g_b, _ = _gates(ba.reshape(1, bs, 2 * n_heads), a_log, dt_bias, False)
    qt = jnp.transpose(act[:, :key_dim].reshape(bs, n_k, HEAD_DIM), (0, 2, 1))
    kt = jnp.transpose(act[:, key_dim:2 * key_dim].reshape(bs, n_k, HEAD_DIM), (0, 2, 1))
    per_head = (bs, n_heads, HEAD_DIM)
    o, new_ssm = _gdn_step(qt, kt, act[:, 2 * key_dim:].reshape(per_head),
                           beta_b.reshape(per_head), g_b.reshape(per_head), z.reshape(per_head),
                           head_g, ssm_states, layer, new_ssm)
    hs = _matmul(o.reshape(bs, value_dim).astype(bf16), w_out16, 0, w_out.shape[1], residual=hs)
    return hp, hs, conv_p, ssm_p, new_conv, new_ssm


def kernel(x_prompt, x_sample, state_pool, state_conv, state_ssm, meta_tokens, norm_g, final_norm_g, pool_w_in, pool_w_grp, pool_scale, pool_w_out, gdn_w_in, gdn_conv_w, gdn_A_log, gdn_dt_bias, gdn_norm_g, gdn_w_out):
    dt = x_prompt.dtype
    batch, seq, d = x_prompt.shape
    depth = norm_g.shape[0]
    meta = jnp.broadcast_to(meta_tokens.astype(dt)[None], (batch, N_META, d))
    hp = jnp.concatenate([jnp.zeros((batch, FRONT, d), dt), meta, x_prompt], axis=1)
    t = hp.shape[1]
    hp = hp.reshape(batch * t, d)
    hs = x_sample.reshape(x_sample.shape[0], d)
    pool_p, conv_p, ssm_p = [], [], []
    pool_s = conv_s = ssm_s = None
    for i in range(depth):
        j = i // 2
        if i % 2 == 0:
            hp, hs, pp, pool_s = _pool_layer(hp, hs, state_pool, j, pool_s, pool_w_in[j],
                                             pool_w_grp[j], pool_scale[j], pool_w_out[j],
                                             norm_g[i], batch)
            pool_p.append(pp)
        else:
            hp, hs, cp, sp, conv_s, ssm_s = _gdn_layer(
                hp, hs, state_conv, state_ssm, j, conv_s, ssm_s, gdn_w_in[j], gdn_conv_w[j],
                gdn_A_log[j], gdn_dt_bias[j], gdn_norm_g[j], gdn_w_out[j], norm_g[i], batch)
            conv_p.append(cp)
            ssm_p.append(sp)
    y_prompt = _final_norm_prompt(hp.reshape(batch, t, d), final_norm_g, seq)
    y_sample = _rmsnorm(hs, final_norm_g, dt).reshape(x_sample.shape)
    return (y_prompt, y_sample, jnp.stack(pool_p), jnp.stack(conv_p), jnp.stack(ssm_p),
            pool_s, conv_s, ssm_s)
```

```python
import functools

import jax
import jax.numpy as jnp
from jax import lax
from jax.experimental import pallas as pl
from jax.experimental.pallas import tpu as pltpu

N_META = 16
PAST_LEN = 16384
POOL_WINDOWS = (2, 4, 8, 16)
POOL_CTX = max(POOL_WINDOWS) - 1
HEAD_DIM = 128
CONV_WIDTH = 4
CHUNK = 64
EPS = 1e-6
FRONT = (-N_META) % CHUNK
QUAD = 4
QUADS_PER_STEP = 4
NEUMANN_LEVELS = 6

VMEM_LIMIT = 56 * 1024 * 1024

f32 = jnp.float32
bf16 = jnp.bfloat16


def _params(sem):
    return pltpu.CompilerParams(dimension_semantics=sem, vmem_limit_bytes=VMEM_LIMIT)


def _silu(x):
    return x * jax.nn.sigmoid(x)


def _stacked_out(stack, layer, shape, dtype):
    if stack is None:
        return jax.ShapeDtypeStruct(shape, dtype), [], [], lambda n_in, n_out: {}
    return (jax.ShapeDtypeStruct(stack.shape, stack.dtype), [stack],
            [pl.BlockSpec(memory_space=pl.ANY)], lambda n_in, n_out: {n_in: n_out})


def _rmsnorm_kernel(x_ref, g_ref, o_ref):
    x = x_ref[...]
    y = x * lax.rsqrt(jnp.mean(x * x, axis=-1, keepdims=True) + EPS)
    o_ref[...] = (y * g_ref[...]).astype(o_ref.dtype)


def _rmsnorm(x, g, out_dtype):
    m, d = x.shape
    tm = 256 if m % 256 == 0 else m
    return pl.pallas_call(
        _rmsnorm_kernel,
        grid=(m // tm,),
        in_specs=[pl.BlockSpec((tm, d), lambda i: (i, 0)),
                  pl.BlockSpec((1, d), lambda i: (0, 0))],
        out_specs=pl.BlockSpec((tm, d), lambda i: (i, 0)),
        out_shape=jax.ShapeDtypeStruct((m, d), out_dtype),
        compiler_params=_params(("parallel",)),
        name="rmsnorm",
    )(x, g.reshape(1, d))


def _final_norm_prompt(h3, g, seq):
    b, t, d = h3.shape
    skip = (t - seq) // CHUNK
    return pl.pallas_call(
        _rmsnorm_kernel,
        grid=(b, seq // CHUNK),
        in_specs=[pl.BlockSpec((None, CHUNK, d), lambda i, j: (i, j + skip, 0)),
                  pl.BlockSpec((1, d), lambda i, j: (0, 0))],
        out_specs=pl.BlockSpec((None, CHUNK, d), lambda i, j: (i, j, 0)),
        out_shape=jax.ShapeDtypeStruct((b, seq, d), h3.dtype),
        compiler_params=_params(("parallel", "parallel")),
        name="final_norm",
    )(h3, g.reshape(1, d))


MM_K = 4096
MM_TN = 512


def _mm_kernel(*refs, epilogue, m_axis):
    x_ref, w_ref = refs[0], refs[1]
    o_ref, w16_ref = refs[-2], refs[-1]

    @pl.when(pl.program_id(m_axis) == 0)
    def _():
        w16_ref[...] = w_ref[...].astype(bf16)

    acc = jnp.dot(x_ref[...], w16_ref[...], preferred_element_type=f32)
    if epilogue == "plain":
        o_ref[...] = acc.astype(o_ref.dtype)
    elif epilogue == "gate":
        scale_ref, z_ref = refs[2], refs[3]
        o_ref[...] = (acc * scale_ref[...] * _silu(z_ref[...])).astype(o_ref.dtype)
    elif epilogue == "residual":
        h_ref = refs[2]
        o_ref[...] = h_ref[...] + acc
    else:
        raise ValueError(epilogue)


def _mm_rows(m, epilogue):
    for tm in ((1056,) if epilogue == "residual" else (1408, 1056)):
        if m % tm == 0:
            return tm
    return m


def _matmul(x, w, layer, col0, ncols, kblock=0, out_dtype=f32, residual=None):
    m = x.shape[0]
    tk = min(MM_K, x.shape[1])
    epilogue = "plain" if residual is None else "residual"
    tm = _mm_rows(m, epilogue)
    tn = min(MM_TN, ncols)
    off = col0 // tn
    assert col0 % tn == 0 and ncols % tn == 0 and x.shape[1] % tk == 0
    in_specs = [pl.BlockSpec((tm, tk), lambda j, i: (i, kblock)),
                pl.BlockSpec((None, tk, tn), lambda j, i: (layer, kblock, j + off))]
    args = [x, w]
    if residual is not None:
        in_specs.append(pl.BlockSpec((tm, tn), lambda j, i: (i, j)))
        args.append(residual)
    return pl.pallas_call(
        functools.partial(_mm_kernel, epilogue=epilogue, m_axis=1),
        grid=(ncols // tn, m // tm),
        in_specs=in_specs,
        out_specs=pl.BlockSpec((tm, tn), lambda j, i: (i, j)),
        out_shape=jax.ShapeDtypeStruct((m, ncols), out_dtype),
        scratch_shapes=[pltpu.VMEM((tk, tn), bf16)],
        compiler_params=_params(("parallel", "arbitrary")),
        name="matmul",
    )(*args)


def _out_project(h, act, w_out, layer):
    tk = min(MM_K, act.shape[1])
    for kb in range(act.shape[1] // tk):
        h = _matmul(act, w_out, layer, 0, w_out.shape[2], kblock=kb, residual=h)
    return h


def _grouped_matmul_gate(d, w_grp, layer, scale, z):
    m, e = d.shape
    _, ng, gk, gn = w_grp.shape
    tm = _mm_rows(m, "gate")
    tn = MM_TN
    nb = gn // tn
    return pl.pallas_call(
        functools.partial(_mm_kernel, epilogue="gate", m_axis=2),
        grid=(ng, nb, m // tm),
        in_specs=[pl.BlockSpec((tm, gk), lambda g, j, i: (i, g)),
                  pl.BlockSpec((None, None, gk, tn), lambda g, j, i: (layer, g, 0, j)),
                  pl.BlockSpec((1, tn), lambda g, j, i: (0, g * nb + j)),
                  pl.BlockSpec((tm, tn), lambda g, j, i: (i, g * nb + j))],
        out_specs=pl.BlockSpec((tm, tn), lambda g, j, i: (i, g * nb + j)),
        out_shape=jax.ShapeDtypeStruct((m, e), bf16),
        scratch_shapes=[pltpu.VMEM((gk, tn), bf16)],
        compiler_params=_params(("parallel", "parallel", "arbitrary")),
        name="grouped_matmul_gate",
    )(d, w_grp, scale.reshape(1, e), z)


POOL_HALO = 16
POOL_ROWS = 16
POOL_LANES = 512


def _pool_diff_prompt_kernel(halo_ref, cur_ref, o_ref, ext_ref, *, tm, gsz):
    t = pl.program_id(1)
    gi = (pl.program_id(2) * POOL_LANES) // gsz
    ext_ref[0:POOL_HALO, :] = jnp.where(t > 0, halo_ref[...], 0.0)
    ext_ref[POOL_HALO:POOL_HALO + tm, :] = cur_ref[...]
    for widx, w in enumerate(POOL_WINDOWS):
        @pl.when(gi == widx)
        def _(w=w):
            for r in range(0, tm, POOL_ROWS):
                base = POOL_HALO + r
                cur = ext_ref[base:base + POOL_ROWS, :]
                s = cur
                for i in range(1, w):
                    s = s + ext_ref[base - i:base - i + POOL_ROWS, :]
                pos = lax.broadcasted_iota(jnp.int32, (POOL_ROWS, 1), 0) + (t * tm + r - FRONT)
                cnt = jnp.clip(pos + 1, 1, w).astype(f32)
                o_ref[r:r + POOL_ROWS, :] = (s / cnt - cur).astype(o_ref.dtype)


def _pool_diff_prompt(u3):
    b, t, e = u3.shape
    tm = 528
    assert t % tm == 0 and tm % POOL_HALO == 0 and e % (len(POOL_WINDOWS) * POOL_LANES) == 0
    hb = tm // POOL_HALO
    return pl.pallas_call(
        functools.partial(_pool_diff_prompt_kernel, tm=tm, gsz=e // len(POOL_WINDOWS)),
        grid=(b, t // tm, e // POOL_LANES),
        in_specs=[pl.BlockSpec((None, POOL_HALO, POOL_LANES),
                               lambda i, j, c: (i, jnp.maximum(j * hb - 1, 0), c)),
                  pl.BlockSpec((None, tm, POOL_LANES), lambda i, j, c: (i, j, c))],
        out_specs=pl.BlockSpec((None, tm, POOL_LANES), lambda i, j, c: (i, j, c)),
        out_shape=jax.ShapeDtypeStruct((b, t, e), bf16),
        scratch_shapes=[pltpu.VMEM((POOL_HALO + tm, POOL_LANES), f32)],
        compiler_params=_params(("parallel", "parallel", "parallel")),
        name="pool_diff_prompt",
    )(u3, u3)


def _pool_diff_sample_kernel(*refs, gsz, lanes):
    st_ref, u_ref = refs[0], refs[1]
    o_ref, ns_ref = refs[-2], refs[-1]
    gi = (pl.program_id(1) * lanes) // gsz
    for widx, w in enumerate(POOL_WINDOWS):
        @pl.when(gi == widx)
        def _(w=w):
            cur = u_ref[...]
            s = cur
            for i in range(1, w):
                s = s + st_ref[:, POOL_CTX - i, :]
            cnt = float(min(PAST_LEN + 1, w))
            o_ref[...] = (s / cnt - cur).astype(o_ref.dtype)
    for r in range(POOL_CTX - 1):
        ns_ref[:, r, :] = st_ref[:, r + 1, :]
    ns_ref[:, POOL_CTX - 1, :] = u_ref[...]


def _pool_diff_sample(states, layer, u, new_states):
    _, b, _, e = states.shape
    gsz = e // len(POOL_WINDOWS)
    bb, lanes = 16, min(1024, gsz)
    assert b % bb == 0 and gsz % lanes == 0
    st_spec = pl.BlockSpec((None, bb, POOL_CTX, lanes), lambda i, c: (layer, i, 0, c))
    ns_shape, extra, extra_specs, alias = _stacked_out(new_states, layer, states.shape, states.dtype)
    return pl.pallas_call(
        functools.partial(_pool_diff_sample_kernel, gsz=gsz, lanes=lanes),
        grid=(b // bb, e // lanes),
        in_specs=[st_spec, pl.BlockSpec((bb, lanes), lambda i, c: (i, c))] + extra_specs,
        out_specs=[pl.BlockSpec((bb, lanes), lambda i, c: (i, c)), st_spec],
        out_shape=[jax.ShapeDtypeStruct((b, e), bf16), ns_shape],
        input_output_aliases=alias(2, 1),
        compiler_params=_params(("parallel", "parallel")),
        name="pool_diff_sample",
    )(states, u, *extra)


CONV_HALO = 8
CONV_ROWS = 24
CONV_LANES = 1024


def _conv_act(acc, kind):
    y = _silu(acc)
    if kind == 2:
        return y
    parts = []
    for h in range(y.shape[1] // HEAD_DIM):
        yh = y[:, h * HEAD_DIM:(h + 1) * HEAD_DIM]
        yh = yh * lax.rsqrt(jnp.sum(yh * yh, axis=-1, keepdims=True) + EPS)
        if kind == 0:
            yh = yh * (HEAD_DIM ** -0.5)
        parts.append(yh)
    return jnp.concatenate(parts, axis=1)


def _conv_kind(c, key_dim):
    return jnp.minimum((c * CONV_LANES) // key_dim, 2)


def _conv_prompt_kernel(halo_ref, cur_ref, w_ref, o_ref, ext_ref, *, tm, key_dim):
    t = pl.program_id(1)
    kind_id = _conv_kind(pl.program_id(2), key_dim)
    ext_ref[0:CONV_HALO, :] = jnp.where(t > 0, halo_ref[...], 0.0)
    ext_ref[CONV_HALO:CONV_HALO + tm, :] = cur_ref[...]
    for kind in range(3):
        @pl.when(kind_id == kind)
        def _(kind=kind):
            for r in range(0, tm, CONV_ROWS):
                base = CONV_HALO + r - (CONV_WIDTH - 1)
                acc = ext_ref[base:base + CONV_ROWS, :] * w_ref[0:1, :]
                for i in range(1, CONV_WIDTH):
                    acc = acc + ext_ref[base + i:base + i + CONV_ROWS, :] * w_ref[i:i + 1, :]
                o_ref[r:r + CONV_ROWS, :] = _conv_act(acc, kind)


def _conv_prompt(x3, conv_w, key_dim):
    b, t, c = x3.shape
    tm = 528
    assert t % tm == 0 and tm % CONV_ROWS == 0 and tm % CONV_HALO == 0 and key_dim % CONV_LANES == 0
    hb = tm // CONV_HALO
    return pl.pallas_call(
        functools.partial(_conv_prompt_kernel, tm=tm, key_dim=key_dim),
        grid=(b, t // tm, c // CONV_LANES),
        in_specs=[pl.BlockSpec((None, CONV_HALO, CONV_LANES),
                               lambda i, j, l: (i, jnp.maximum(j * hb - 1, 0), l)),
                  pl.BlockSpec((None, tm, CONV_LANES), lambda i, j, l: (i, j, l)),
                  pl.BlockSpec((CONV_WIDTH, CONV_LANES), lambda i, j, l: (0, l))],
        out_specs=pl.BlockSpec((None, tm, CONV_LANES), lambda i, j, l: (i, j, l)),
        out_shape=jax.ShapeDtypeStruct((b, t, c), f32),
        scratch_shapes=[pltpu.VMEM((CONV_HALO + tm, CONV_LANES), f32)],
        compiler_params=_params(("parallel", "parallel", "parallel")),
        name="conv_prompt",
    )(x3, x3, conv_w)


def _conv_sample_kernel(*refs, key_dim):
    st_ref, x_ref, w_ref = refs[0], refs[1], refs[2]
    o_ref, ns_ref = refs[-2], refs[-1]
    kind_id = _conv_kind(pl.program_id(0), key_dim)
    for kind in range(3):
        @pl.when(kind_id == kind)
        def _(kind=kind):
            acc = x_ref[...] * w_ref[CONV_WIDTH - 1:CONV_WIDTH, :]
            for i in range(CONV_WIDTH - 1):
                acc = acc + st_ref[:, i, :] * w_ref[i:i + 1, :]
            o_ref[...] = _conv_act(acc, kind)
    for r in range(CONV_WIDTH - 2):
        ns_ref[:, r, :] = st_ref[:, r + 1, :]
    ns_ref[:, CONV_WIDTH - 2, :] = x_ref[...]


def _conv_sample(states, layer, x, conv_w, key_dim, new_states):
    b, c = x.shape
    st_spec = pl.BlockSpec((None, b, CONV_WIDTH - 1, CONV_LANES), lambda l: (layer, 0, 0, l))
    ns_shape, extra, extra_specs, alias = _stacked_out(new_states, layer, states.shape, states.dtype)
    return pl.pallas_call(
        functools.partial(_conv_sample_kernel, key_dim=key_dim),
        grid=(c // CONV_LANES,),
        in_specs=[st_spec,
                  pl.BlockSpec((b, CONV_LANES), lambda l: (0, l)),
                  pl.BlockSpec((CONV_WIDTH, CONV_LANES), lambda l: (0, l))] + extra_specs,
        out_specs=[pl.BlockSpec((b, CONV_LANES), lambda l: (0, l)), st_spec],
        out_shape=[jax.ShapeDtypeStruct((b, c), f32), ns_shape],
        input_output_aliases=alias(3, 1),
        compiler_params=_params(("parallel",)),
        name="conv_sample",
    )(states, x, conv_w, *extra)


def _gate_kernel(ba_ref, a_ref, dt_ref, beta_ref, gcb_ref, gc_ref, *, rows, n_heads, cumulative):
    x = ba_ref[...]
    beta = jax.nn.sigmoid(x)
    xa = x + dt_ref[...]
    softplus = jnp.maximum(xa, 0.0) + jnp.log1p(jnp.exp(-jnp.abs(xa)))
    g = -jnp.exp(a_ref[...]) * softplus
    if cumulative:
        row = lax.broadcasted_iota(jnp.int32, (rows, 1), 0)
        g = jnp.where(row < jnp.where(pl.program_id(1) == 0, FRONT, 0), 0.0, g)
        shift = 1
        while shift < rows:
            g = g + jnp.where(row >= shift, pltpu.roll(g, shift, 0), 0.0)
            shift *= 2
    gc_ref[...] = g
    for h in range(n_heads):
        sl = slice(h * HEAD_DIM, (h + 1) * HEAD_DIM)
        beta_ref[:, sl] = jnp.broadcast_to(beta[:, h:h + 1], (rows, HEAD_DIM))
        gcb_ref[:, sl] = jnp.broadcast_to(g[:, n_heads + h:n_heads + h + 1], (rows, HEAD_DIM))


def _gates(ba3, a_log, dt_bias, cumulative):
    b, t, w = ba3.shape
    n_heads = w // 2
    rows = CHUNK if cumulative else t
    zeros = jnp.zeros((n_heads,), f32)
    a_pad = jnp.concatenate([zeros, a_log.astype(f32)]).reshape(1, w)
    dt_pad = jnp.concatenate([zeros, dt_bias.astype(f32)]).reshape(1, w)
    wide = n_heads * HEAD_DIM
    return pl.pallas_call(
        functools.partial(_gate_kernel, rows=rows, n_heads=n_heads, cumulative=cumulative),
        grid=(b, t // rows),
        in_specs=[pl.BlockSpec((None, rows, w), lambda i, j: (i, j, 0)),
                  pl.BlockSpec((1, w), lambda i, j: (0, 0)),
                  pl.BlockSpec((1, w), lambda i, j: (0, 0))],
        out_specs=[pl.BlockSpec((None, rows, wide), lambda i, j: (i, j, 0)),
                   pl.BlockSpec((None, rows, wide), lambda i, j: (i, j, 0)),
                   pl.BlockSpec((None, rows, w), lambda i, j: (i, j, 0))],
        out_shape=[jax.ShapeDtypeStruct((b, t, wide), f32),
                   jax.ShapeDtypeStruct((b, t, wide), f32),
                   jax.ShapeDtypeStruct((b, t, w), f32)],
        compiler_params=_params(("parallel", "parallel")),
        name="gdn_gates",
    )(ba3, a_pad, dt_pad)


def _dot_nt(a, b):
    return lax.dot_general(a, b, (((1,), (1,)), ((), ())), preferred_element_type=f32)


def _dot_tn(a, b):
    return lax.dot_general(a, b, (((0,), (0,)), ((), ())), preferred_element_type=f32)


def _dot16(a, b):
    return jnp.dot(a.astype(bf16), b.astype(bf16), preferred_element_type=f32)


def _dot_split(a, b):
    a_hi = a.astype(bf16)
    a_lo = (a - a_hi.astype(f32)).astype(bf16)
    b_hi = b.astype(bf16)
    b_lo = (b - b_hi.astype(f32)).astype(bf16)
    return (jnp.dot(a_hi, b_hi, preferred_element_type=f32)
            + (jnp.dot(a_hi, b_lo, preferred_element_type=f32)
               + jnp.dot(a_lo, b_hi, preferred_element_type=f32)))


def _stack(x, idx):
    return jnp.concatenate([x[:, i * HEAD_DIM:(i + 1) * HEAD_DIM] for i in idx], axis=0)


def _gdn_chunk_kernel(q_ref, k_ref, v_ref, beta_ref, gcb_ref, gcr_ref, z_ref, ng_ref,
                      o_ref, sout_ref, s_ref):
    c = pl.program_id(2)

    @pl.when(c == 0)
    def _():
        s_ref[...] = jnp.zeros_like(s_ref)

    quads = range(QUADS_PER_STEP)
    heads = tuple(range(QUAD))
    khead = tuple(h // 2 for h in heads)
    rows = QUAD * CHUNK
    kq_w = QUAD // 2 * HEAD_DIM
    v_w = QUAD * HEAD_DIM

    def head_rows(h):
        return slice(h * CHUNK, (h + 1) * CHUNK)

    def head_lanes(h):
        return slice(h * HEAD_DIM, (h + 1) * HEAD_DIM)

    ri = lax.broadcasted_iota(jnp.int32, (rows, rows), 0)
    ci = lax.broadcasted_iota(jnp.int32, (rows, rows), 1)
    shift = CHUNK.bit_length() - 1
    same = lax.shift_right_logical(ri, shift) == lax.shift_right_logical(ci, shift)
    lower = same & (ri >= ci)
    strict = ri > ci

    k16, kb16, q16, rhs, qd, kd16, decay, g_last = [], [], [], [], [], [], [], []
    for i in quads:
        kq = slice(i * kq_w, (i + 1) * kq_w)
        vs = slice(i * v_w, (i + 1) * v_w)
        gcb = gcb_ref[:, vs]
        kst = _stack(k_ref[:, kq], khead)
        qst = _stack(q_ref[:, kq], khead)
        bst = _stack(beta_ref[:, vs], heads)
        gst = _stack(gcb, heads)
        last = [gcb[CHUNK - 1:CHUNK, head_lanes(h)] for h in heads]
        glast = jnp.concatenate([jnp.broadcast_to(r, (CHUNK, HEAD_DIM)) for r in last], axis=0)
        eg = jnp.exp(gst)
        kb = kst * bst
        k16.append(kst.astype(bf16))
        kb16.append(kb.astype(bf16))
        q16.append(qst.astype(bf16))
        rhs.append(jnp.concatenate([_stack(v_ref[:, vs], heads) * bst, kb * eg], axis=1))
        qd.append(qst * eg)
        kd16.append((kst * jnp.exp(glast - gst)).astype(bf16))
        diff = jnp.concatenate([gst] * (rows // HEAD_DIM), axis=1) - gcr_ref[i]
        decay.append(jnp.exp(jnp.where(lower, diff, -jnp.inf)))
        g_last.append([jnp.exp(r) for r in last])

    pw = [-jnp.where(strict, _dot_nt(kb16[i], k16[i]) * decay[i], 0.0) for i in quads]
    attn16 = [(_dot_nt(q16[i], k16[i]) * decay[i]).astype(bf16) for i in quads]

    sol = [rhs[i] + _dot_split(pw[i], rhs[i]) for i in quads]
    for _ in range(NEUMANN_LEVELS - 1):
        pw16 = [p.astype(bf16) for p in pw]
        pw = [jnp.dot(p, p, preferred_element_type=f32) for p in pw16]
        sol = [sol[i] + _dot16(pw[i], sol[i]) for i in quads]

    x = [[_dot16(jnp.concatenate([sol[i][head_rows(h), HEAD_DIM:], qd[i][head_rows(h)]], axis=0),
                 s_ref[i * QUAD + h]) for h in heads] for i in quads]
    vn16 = [jnp.concatenate([sol[i][head_rows(h), :HEAD_DIM] - x[i][h][:CHUNK] for h in heads],
                            axis=0).astype(bf16) for i in quads]
    o = [jnp.concatenate([x[i][h][CHUNK:] for h in heads], axis=0)
         + jnp.dot(attn16[i], vn16[i], preferred_element_type=f32) for i in quads]
    for i in quads:
        for h in heads:
            s_ref[i * QUAD + h] = (s_ref[i * QUAD + h] * g_last[i][h]
                                   + _dot_tn(kd16[i][head_rows(h)], vn16[i][head_rows(h)]))

    for i in quads:
        on = o[i] * lax.rsqrt(jnp.mean(o[i] * o[i], axis=-1, keepdims=True) + EPS) * ng_ref[...]
        for h in heads:
            sl = slice(i * v_w + h * HEAD_DIM, i * v_w + (h + 1) * HEAD_DIM)
            o_ref[:, sl] = (on[head_rows(h)] * _silu(z_ref[:, sl])).astype(o_ref.dtype)

    @pl.when(c == pl.num_programs(2) - 1)
    def _():
        sout_ref[...] = s_ref[...]


def _gdn_chunked(qkv, beta_b, gc_b, gc_rows, z, norm_g, batch):
    m, value_dim = z.shape
    key_dim = (qkv.shape[1] - value_dim) // 2
    n_heads = value_dim // HEAD_DIM
    n_chunks = m // batch // CHUNK
    heads_per_step = QUAD * QUADS_PER_STEP
    n_groups = n_heads // heads_per_step
    kq_w = heads_per_step // 2 * HEAD_DIM
    v_w = heads_per_step * HEAD_DIM

    def row_spec(width, col0=0):
        off = col0 // width
        return pl.BlockSpec((CHUNK, width), lambda b, g, c: (b * n_chunks + c, g + off))

    return pl.pallas_call(
        _gdn_chunk_kernel,
        grid=(batch, n_groups, n_chunks),
        in_specs=[row_spec(kq_w), row_spec(kq_w, key_dim), row_spec(v_w, 2 * key_dim),
                  row_spec(v_w), row_spec(v_w),
                  pl.BlockSpec((None, None, QUADS_PER_STEP, 1, QUAD * CHUNK),
                               lambda b, g, c: (b, c, g, 0, 0)),
                  row_spec(v_w),
                  pl.BlockSpec((1, HEAD_DIM), lambda b, g, c: (0, 0))],
        out_specs=[row_spec(v_w),
                   pl.BlockSpec((None, heads_per_step, HEAD_DIM, HEAD_DIM),
                                lambda b, g, c: (b, g, 0, 0))],
        out_shape=[jax.ShapeDtypeStruct((m, value_dim), bf16),
                   jax.ShapeDtypeStruct((batch, n_heads, HEAD_DIM, HEAD_DIM), f32)],
        scratch_shapes=[pltpu.VMEM((heads_per_step, HEAD_DIM, HEAD_DIM), f32)],
        compiler_params=_params(("parallel", "parallel", "arbitrary")),
        name="gdn_chunked",
    )(qkv, qkv, qkv, beta_b, gc_b, gc_rows, z, norm_g.reshape(1, HEAD_DIM))


def _gdn_step_kernel(*refs, n_heads):
    qt_ref, kt_ref, v_ref, beta_ref, g_ref, z_ref, ng_ref, s_ref = refs[:8]
    o_ref, sout_ref, kcol_ref, qcol_ref, eg_ref, oraw_ref = refs[-6:]
    n_k = qt_ref.shape[1]
    rep = n_heads // n_k
    for kh in range(n_k):
        kcol_ref[kh] = jnp.broadcast_to(kt_ref[:, kh:kh + 1], (HEAD_DIM, HEAD_DIM))
        qcol_ref[kh] = jnp.broadcast_to(qt_ref[:, kh:kh + 1], (HEAD_DIM, HEAD_DIM))
    eg_ref[...] = jnp.exp(g_ref[...])
    for h in range(n_heads):
        kcol = kcol_ref[h // rep]
        s = s_ref[h] * eg_ref[h:h + 1, :]
        kv = jnp.sum(kcol * s, axis=0, keepdims=True)
        delta = (v_ref[h:h + 1, :] - kv) * beta_ref[h:h + 1, :]
        s = s + kcol * delta
        sout_ref[h] = s
        oraw_ref[h:h + 1, :] = jnp.sum(qcol_ref[h // rep] * s, axis=0, keepdims=True)
    o = oraw_ref[...]
    on = o * lax.rsqrt(jnp.mean(o * o, axis=-1, keepdims=True) + EPS) * ng_ref[...]
    o_ref[...] = on * _silu(z_ref[...])


def _gdn_step(qt, kt, v3, beta3, g3, z3, norm_g, states, layer, new_states):
    b, n_heads, _ = v3.shape
    n_k = qt.shape[2]

    def head_spec():
        return pl.BlockSpec((None, n_heads, HEAD_DIM), lambda i: (i, 0, 0))

    def t_spec():
        return pl.BlockSpec((None, HEAD_DIM, n_k), lambda i: (i, 0, 0))

    s_spec = pl.BlockSpec((None, None, n_heads, HEAD_DIM, HEAD_DIM),
                          lambda i: (layer, i, 0, 0, 0))
    ns_shape, extra, extra_specs, alias = _stacked_out(new_states, layer, states.shape, f32)
    return pl.pallas_call(
        functools.partial(_gdn_step_kernel, n_heads=n_heads),
        grid=(b,),
        in_specs=[t_spec(), t_spec(), head_spec(), head_spec(), head_spec(), head_spec(),
                  pl.BlockSpec((1, HEAD_DIM), lambda i: (0, 0)), s_spec] + extra_specs,
        out_specs=[head_spec(), s_spec],
        out_shape=[jax.ShapeDtypeStruct((b, n_heads, HEAD_DIM), f32), ns_shape],
        scratch_shapes=[pltpu.VMEM((n_k, HEAD_DIM, HEAD_DIM), f32),
                        pltpu.VMEM((n_k, HEAD_DIM, HEAD_DIM), f32),
                        pltpu.VMEM((n_heads, HEAD_DIM), f32),
                        pltpu.VMEM((n_heads, HEAD_DIM), f32)],
        input_output_aliases=alias(8, 1),
        compiler_params=_params(("parallel",)),
        name="gdn_step",
    )(qt, kt, v3, beta3, g3, z3, norm_g.reshape(1, HEAD_DIM), states, *extra)


def _pool_layer(hp, hs, states, layer, new_states, w_in, w_grp, scale, w_out, norm_g, batch):
    e = w_grp.shape[1] * w_grp.shape[2]

    def project(h):
        xn = _rmsnorm(h, norm_g, bf16)
        return _matmul(xn, w_in, layer, 0, e), _matmul(xn, w_in, layer, e, e)

    def mix(h, d, z):
        return _out_project(h, _grouped_matmul_gate(d, w_grp, layer, scale, z), w_out, layer)

    u, z = project(hp)
    u3 = u.reshape(batch, -1, e)
    hp = mix(hp, _pool_diff_prompt(u3).reshape(u.shape), z)
    pool_p = u3[:, -POOL_CTX:]

    u, z = project(hs)
    d, new_states = _pool_diff_sample(states, layer, u, new_states)
    hs = mix(hs, d, z)
    return hp, hs, pool_p, new_states


def _gdn_layer(hp, hs, conv_states, ssm_states, layer, new_conv, new_ssm, w_in, conv_w, a_log,
               dt_bias, head_g, w_out, norm_g, batch):
    value_dim = w_out.shape[1]
    n_heads = value_dim // HEAD_DIM
    conv_dim = conv_w.shape[1]
    key_dim = (conv_dim - value_dim) // 2
    n_k = key_dim // HEAD_DIM

    def project(h):
        xn = _rmsnorm(h, norm_g, bf16)
        qkv = _matmul(xn, w_in, layer, 0, conv_dim)
        z = _matmul(xn, w_in, layer, conv_dim, value_dim)
        ba = _matmul(xn, w_in, layer, conv_dim + value_dim, 2 * n_heads)
        return qkv, z, ba

    qkv, z, ba = project(hp)
    t = hp.shape[0] // batch
    qkv3 = qkv.reshape(batch, t, conv_dim)
    act = _conv_prompt(qkv3, conv_w, key_dim).reshape(-1, conv_dim)
    beta_b, gc_b, gc = _gates(ba.reshape(batch, t, 2 * n_heads), a_log, dt_bias, True)
    gc_rows = gc[..., n_heads:].reshape(batch, t // CHUNK, CHUNK, n_heads // QUAD, QUAD)
    gc_rows = jnp.transpose(gc_rows, (0, 1, 3, 4, 2)).reshape(
        batch, t // CHUNK, n_heads // QUAD, 1, QUAD * CHUNK)
    o, ssm_p = _gdn_chunked(act, beta_b.reshape(-1, value_dim), gc_b.reshape(-1, value_dim),
                            gc_rows, z, head_g, batch)
    hp = _out_project(hp, o, w_out, layer)
    conv_p = qkv3[:, -(CONV_WIDTH - 1):]

    qkv, z, ba = project(hs)
    bs = hs.shape[0]
    act, new_conv = _conv_sample(conv_states, layer, qkv, conv_w, key_dim, new_conv)
    beta_b, g_b, _ = _gates(ba.reshape(1, bs, 2 * n_heads), a_log, dt_bias, False)
    qt = jnp.transpose(act[:, :key_dim].reshape(bs, n_k, HEAD_DIM), (0, 2, 1))
    kt = jnp.transpose(act[:, key_dim:2 * key_dim].reshape(bs, n_k, HEAD_DIM), (0, 2, 1))
    per_head = (bs, n_heads, HEAD_DIM)
    o, new_ssm = _gdn_step(qt, kt, act[:, 2 * key_dim:].reshape(per_head),
                           beta_b.reshape(per_head), g_b.reshape(per_head), z.reshape(per_head),
                           head_g, ssm_states, layer, new_ssm)
    hs = _out_project(hs, o.reshape(bs, value_dim).astype(bf16), w_out, layer)
    return hp, hs, conv_p, ssm_p, new_conv, new_ssm


def kernel(x_prompt, x_sample, state_pool, state_conv, state_ssm, meta_tokens, norm_g, final_norm_g, pool_w_in, pool_w_grp, pool_scale, pool_w_out, gdn_w_in, gdn_conv_w, gdn_A_log, gdn_dt_bias, gdn_norm_g, gdn_w_out):
    dt = x_prompt.dtype
    batch, seq, d = x_prompt.shape
    depth = norm_g.shape[0]
    meta = jnp.broadcast_to(meta_tokens.astype(dt)[None], (batch, N_META, d))
    hp = jnp.concatenate([jnp.zeros((batch, FRONT, d), dt), meta, x_prompt], axis=1)
    t = hp.shape[1]
    hp = hp.reshape(batch * t, d)
    hs = x_sample.reshape(x_sample.shape[0], d)
    pool_p, conv_p, ssm_p = [], [], []
    pool_s = conv_s = ssm_s = None
    for i in range(depth):
        j = i // 2
        if i % 2 == 0:
            hp, hs, pp, pool_s = _pool_layer(hp, hs, state_pool, j, pool_s, pool_w_in, pool_w_grp,
                                             pool_scale[j], pool_w_out, norm_g[i], batch)
            pool_p.append(pp)
        else:
            hp, hs, cp, sp, conv_s, ssm_s = _gdn_layer(
                hp, hs, state_conv, state_ssm, j, conv_s, ssm_s, gdn_w_in, gdn_conv_w[j],
                gdn_A_log[j], gdn_dt_bias[j], gdn_norm_g[j], gdn_w_out, norm_g[i], batch)
            conv_p.append(cp)
            ssm_p.append(sp)
    y_prompt = _final_norm_prompt(hp.reshape(batch, t, d), final_norm_g, seq)
    y_sample = _rmsnorm(hs, final_norm_g, dt).reshape(x_sample.shape)
    return (y_prompt, y_sample, jnp.stack(pool_p), jnp.stack(conv_p), jnp.stack(ssm_p),
            pool_s, conv_s, ssm_s)
```

```python
import functools

import jax
import jax.numpy as jnp
from jax import lax
from jax.experimental import pallas as pl
from jax.experimental.pallas import tpu as pltpu

N_META = 16
PAST_LEN = 16384
POOL_WINDOWS = (2, 4, 8, 16)
POOL_CTX = max(POOL_WINDOWS) - 1
HEAD_DIM = 128
CONV_WIDTH = 4
CHUNK = 64
EPS = 1e-6
FRONT = (-N_META) % CHUNK
QUAD = 4
QUADS_PER_STEP = 8
NEUMANN_LEVELS = 6

VMEM_LIMIT = 56 * 1024 * 1024
SUBLANES = 8

f32 = jnp.float32
bf16 = jnp.bfloat16


def _params(sem):
    return pltpu.CompilerParams(dimension_semantics=sem, vmem_limit_bytes=VMEM_LIMIT)


def _silu(x):
    return x * jax.nn.sigmoid(x)


def _stacked_out(stack, layer, shape, dtype):
    if stack is None:
        return jax.ShapeDtypeStruct(shape, dtype), [], [], lambda n_in, n_out: {}
    return (jax.ShapeDtypeStruct(stack.shape, stack.dtype), [stack],
            [pl.BlockSpec(memory_space=pl.ANY)], lambda n_in, n_out: {n_in: n_out})


def _rmsnorm_kernel(x_ref, g_ref, o_ref):
    x = x_ref[...]
    y = x * lax.rsqrt(jnp.mean(x * x, axis=-1, keepdims=True) + EPS)
    o_ref[...] = (y * g_ref[...]).astype(o_ref.dtype)


def _rmsnorm(x, g, out_dtype):
    m, d = x.shape
    tm = 256 if m % 256 == 0 else m
    return pl.pallas_call(
        _rmsnorm_kernel,
        grid=(m // tm,),
        in_specs=[pl.BlockSpec((tm, d), lambda i: (i, 0)),
                  pl.BlockSpec((1, d), lambda i: (0, 0))],
        out_specs=pl.BlockSpec((tm, d), lambda i: (i, 0)),
        out_shape=jax.ShapeDtypeStruct((m, d), out_dtype),
        compiler_params=_params(("parallel",)),
        name="rmsnorm",
    )(x, g.reshape(1, d))


def _final_norm_prompt(h3, g, seq):
    b, t, d = h3.shape
    skip = (t - seq) // CHUNK
    return pl.pallas_call(
        _rmsnorm_kernel,
        grid=(b, seq // CHUNK),
        in_specs=[pl.BlockSpec((None, CHUNK, d), lambda i, j: (i, j + skip, 0)),
                  pl.BlockSpec((1, d), lambda i, j: (0, 0))],
        out_specs=pl.BlockSpec((None, CHUNK, d), lambda i, j: (i, j, 0)),
        out_shape=jax.ShapeDtypeStruct((b, seq, d), h3.dtype),
        compiler_params=_params(("parallel", "parallel")),
        name="final_norm",
    )(h3, g.reshape(1, d))


MM_K = 4096
MM_TN = 512


def _mm_kernel(*refs, epilogue, m_axis):
    x_ref, w_ref = refs[0], refs[1]
    o_ref, w16_ref = refs[-2], refs[-1]

    @pl.when(pl.program_id(m_axis) == 0)
    def _():
        w16_ref[...] = w_ref[...].astype(bf16)

    acc = jnp.dot(x_ref[...], w16_ref[...], preferred_element_type=f32)
    if epilogue == "plain":
        o_ref[...] = acc.astype(o_ref.dtype)
    elif epilogue == "gate":
        scale_ref, z_ref = refs[2], refs[3]
        o_ref[...] = (acc * scale_ref[...] * _silu(z_ref[...])).astype(o_ref.dtype)
    elif epilogue == "residual":
        h_ref = refs[2]
        o_ref[...] = h_ref[...] + acc
    else:
        raise ValueError(epilogue)


def _mm_rows(m, epilogue):
    for tm in ((1056,) if epilogue == "residual" else (1408, 1056)):
        if m % tm == 0:
            return tm
    return m


def _matmul(x, w, layer, col0, ncols, kblock=0, out_dtype=f32, residual=None):
    m = x.shape[0]
    tk = min(MM_K, x.shape[1])
    epilogue = "plain" if residual is None else "residual"
    tm = _mm_rows(m, epilogue)
    tn = min(MM_TN, ncols)
    off = col0 // tn
    assert col0 % tn == 0 and ncols % tn == 0 and x.shape[1] % tk == 0
    in_specs = [pl.BlockSpec((tm, tk), lambda j, i: (i, kblock)),
                pl.BlockSpec((None, tk, tn), lambda j, i: (layer, kblock, j + off))]
    args = [x, w]
    if residual is not None:
        in_specs.append(pl.BlockSpec((tm, tn), lambda j, i: (i, j)))
        args.append(residual)
    return pl.pallas_call(
        functools.partial(_mm_kernel, epilogue=epilogue, m_axis=1),
        grid=(ncols // tn, m // tm),
        in_specs=in_specs,
        out_specs=pl.BlockSpec((tm, tn), lambda j, i: (i, j)),
        out_shape=jax.ShapeDtypeStruct((m, ncols), out_dtype),
        scratch_shapes=[pltpu.VMEM((tk, tn), bf16)],
        compiler_params=_params(("parallel", "arbitrary")),
        name="matmul",
    )(*args)


def _out_project(h, act, w_out, layer):
    tk = min(MM_K, act.shape[1])
    for kb in range(act.shape[1] // tk):
        h = _matmul(act, w_out, layer, 0, w_out.shape[2], kblock=kb, residual=h)
    return h


def _grouped_matmul_gate(d, w_grp, layer, scale, z):
    m, e = d.shape
    _, ng, gk, gn = w_grp.shape
    tm = _mm_rows(m, "gate")
    tn = MM_TN
    nb = gn // tn
    return pl.pallas_call(
        functools.partial(_mm_kernel, epilogue="gate", m_axis=2),
        grid=(ng, nb, m // tm),
        in_specs=[pl.BlockSpec((tm, gk), lambda g, j, i: (i, g)),
                  pl.BlockSpec((None, None, gk, tn), lambda g, j, i: (layer, g, 0, j)),
                  pl.BlockSpec((1, tn), lambda g, j, i: (0, g * nb + j)),
                  pl.BlockSpec((tm, tn), lambda g, j, i: (i, g * nb + j))],
        out_specs=pl.BlockSpec((tm, tn), lambda g, j, i: (i, g * nb + j)),
        out_shape=jax.ShapeDtypeStruct((m, e), bf16),
        scratch_shapes=[pltpu.VMEM((gk, tn), bf16)],
        compiler_params=_params(("parallel", "parallel", "arbitrary")),
        name="grouped_matmul_gate",
    )(d, w_grp, scale.reshape(1, e), z)


POOL_HALO = 16
POOL_ROWS = 16
POOL_LANES = 512


def _row_pieces(x):
    return [x[r:r + SUBLANES] for r in range(0, x.shape[0], SUBLANES)]


def _shift_rows(pieces, d):
    if d == SUBLANES:
        return [pieces[0]] + pieces[:-1]
    rolled = [pltpu.roll(p, d, 0) for p in pieces]
    keep = lax.broadcasted_iota(jnp.int32, pieces[0].shape, 0) >= d
    return [rolled[0]] + [jnp.where(keep, rolled[k], rolled[k - 1]) for k in range(1, len(pieces))]


def _window_sums(pieces, w):
    d = 1
    while d < w:
        pieces = [a + b for a, b in zip(pieces, _shift_rows(pieces, d))]
        d *= 2
    return pieces


def _pool_diff_prompt_kernel(halo_ref, cur_ref, o_ref, *, tm, gsz):
    t = pl.program_id(1)
    gi = (pl.program_id(2) * POOL_LANES) // gsz
    for widx, w in enumerate(POOL_WINDOWS):
        ctx = -(-(w - 1) // SUBLANES) * SUBLANES

        @pl.when(gi == widx)
        def _(w=w, ctx=ctx):
            for r in range(0, tm, POOL_ROWS):
                cur = cur_ref[r:r + POOL_ROWS, :]
                if r == 0:
                    prev = jnp.where(t > 0, halo_ref[POOL_HALO - ctx:POOL_HALO, :], 0.0)
                else:
                    prev = cur_ref[r - ctx:r, :]
                s = _window_sums(_row_pieces(prev) + _row_pieces(cur), w)[ctx // SUBLANES:]
                s = jnp.concatenate(s, axis=0)
                pos = lax.broadcasted_iota(jnp.int32, (POOL_ROWS, 1), 0) + (t * tm + r - FRONT)
                cnt = jnp.clip(pos + 1, 1, w).astype(f32)
                o_ref[r:r + POOL_ROWS, :] = (s / cnt - cur).astype(o_ref.dtype)


def _pool_diff_prompt(u3):
    b, t, e = u3.shape
    tm = 528
    assert t % tm == 0 and tm % POOL_HALO == 0 and e % (len(POOL_WINDOWS) * POOL_LANES) == 0
    hb = tm // POOL_HALO
    return pl.pallas_call(
        functools.partial(_pool_diff_prompt_kernel, tm=tm, gsz=e // len(POOL_WINDOWS)),
        grid=(b, t // tm, e // POOL_LANES),
        in_specs=[pl.BlockSpec((None, POOL_HALO, POOL_LANES),
                               lambda i, j, c: (i, jnp.maximum(j * hb - 1, 0), c)),
                  pl.BlockSpec((None, tm, POOL_LANES), lambda i, j, c: (i, j, c))],
        out_specs=pl.BlockSpec((None, tm, POOL_LANES), lambda i, j, c: (i, j, c)),
        out_shape=jax.ShapeDtypeStruct((b, t, e), bf16),
        compiler_params=_params(("parallel", "parallel", "parallel")),
        name="pool_diff_prompt",
    )(u3, u3)


def _pool_diff_sample_kernel(*refs, gsz, lanes):
    st_ref, u_ref = refs[0], refs[1]
    o_ref, ns_ref = refs[-2], refs[-1]
    gi = (pl.program_id(1) * lanes) // gsz
    for widx, w in enumerate(POOL_WINDOWS):
        @pl.when(gi == widx)
        def _(w=w):
            cur = u_ref[...]
            s = cur
            for i in range(1, w):
                s = s + st_ref[:, POOL_CTX - i, :]
            cnt = float(min(PAST_LEN + 1, w))
            o_ref[...] = (s / cnt - cur).astype(o_ref.dtype)
    for r in range(POOL_CTX - 1):
        ns_ref[:, r, :] = st_ref[:, r + 1, :]
    ns_ref[:, POOL_CTX - 1, :] = u_ref[...]


def _pool_diff_sample(states, layer, u, new_states):
    _, b, _, e = states.shape
    gsz = e // len(POOL_WINDOWS)
    bb, lanes = 16, min(1024, gsz)
    assert b % bb == 0 and gsz % lanes == 0
    st_spec = pl.BlockSpec((None, bb, POOL_CTX, lanes), lambda i, c: (layer, i, 0, c))
    ns_shape, extra, extra_specs, alias = _stacked_out(new_states, layer, states.shape, states.dtype)
    return pl.pallas_call(
        functools.partial(_pool_diff_sample_kernel, gsz=gsz, lanes=lanes),
        grid=(b // bb, e // lanes),
        in_specs=[st_spec, pl.BlockSpec((bb, lanes), lambda i, c: (i, c))] + extra_specs,
        out_specs=[pl.BlockSpec((bb, lanes), lambda i, c: (i, c)), st_spec],
        out_shape=[jax.ShapeDtypeStruct((b, e), bf16), ns_shape],
        input_output_aliases=alias(2, 1),
        compiler_params=_params(("parallel", "parallel")),
        name="pool_diff_sample",
    )(states, u, *extra)


CONV_HALO = 8
CONV_ROWS = 16
CONV_LANES = 1024


def _conv_act(acc, scale):
    y = _silu(acc)
    if scale is None:
        return y
    parts = []
    for h in range(y.shape[1] // HEAD_DIM):
        yh = y[:, h * HEAD_DIM:(h + 1) * HEAD_DIM]
        parts.append(yh * lax.rsqrt(jnp.sum(yh * yh, axis=-1, keepdims=True) + EPS) * scale)
    return jnp.concatenate(parts, axis=1)


def _conv_branches(c, key_dim, body):
    normalise = c * CONV_LANES < 2 * key_dim
    q_scale = jnp.where(c * CONV_LANES < key_dim, HEAD_DIM ** -0.5, 1.0)
    pl.when(normalise)(lambda: body(q_scale))
    pl.when(jnp.logical_not(normalise))(lambda: body(None))


def _conv_prompt_kernel(halo_ref, cur_ref, w_ref, o_ref, *, tm, key_dim):
    t = pl.program_id(1)

    def body(scale):
        for r in range(0, tm, CONV_ROWS):
            if r == 0:
                prev = jnp.where(t > 0, halo_ref[...], 0.0)
            else:
                prev = cur_ref[r - CONV_HALO:r, :]
            cur = cur_ref[r:r + CONV_ROWS, :]
            pieces = _row_pieces(prev) + _row_pieces(cur)
            acc = cur * w_ref[CONV_WIDTH - 1:CONV_WIDTH, :]
            for d in range(1, CONV_WIDTH):
                tap = CONV_WIDTH - 1 - d
                back = jnp.concatenate(_shift_rows(pieces, d)[CONV_HALO // SUBLANES:], axis=0)
                acc = acc + back * w_ref[tap:tap + 1, :]
            o_ref[r:r + CONV_ROWS, :] = _conv_act(acc, scale)

    _conv_branches(pl.program_id(2), key_dim, body)


def _conv_prompt(x3, conv_w, key_dim):
    b, t, c = x3.shape
    tm = 528
    assert t % tm == 0 and tm % CONV_ROWS == 0 and tm % CONV_HALO == 0 and key_dim % CONV_LANES == 0
    hb = tm // CONV_HALO
    return pl.pallas_call(
        functools.partial(_conv_prompt_kernel, tm=tm, key_dim=key_dim),
        grid=(b, t // tm, c // CONV_LANES),
        in_specs=[pl.BlockSpec((None, CONV_HALO, CONV_LANES),
                               lambda i, j, l: (i, jnp.maximum(j * hb - 1, 0), l)),
                  pl.BlockSpec((None, tm, CONV_LANES), lambda i, j, l: (i, j, l)),
                  pl.BlockSpec((CONV_WIDTH, CONV_LANES), lambda i, j, l: (0, l))],
        out_specs=pl.BlockSpec((None, tm, CONV_LANES), lambda i, j, l: (i, j, l)),
        out_shape=jax.ShapeDtypeStruct((b, t, c), f32),
        compiler_params=_params(("parallel", "parallel", "parallel")),
        name="conv_prompt",
    )(x3, x3, conv_w)


def _conv_sample_kernel(*refs, key_dim):
    st_ref, x_ref, w_ref = refs[0], refs[1], refs[2]
    o_ref, ns_ref = refs[-2], refs[-1]

    def body(scale):
        acc = x_ref[...] * w_ref[CONV_WIDTH - 1:CONV_WIDTH, :]
        for i in range(CONV_WIDTH - 1):
            acc = acc + st_ref[:, i, :] * w_ref[i:i + 1, :]
        o_ref[...] = _conv_act(acc, scale)

    _conv_branches(pl.program_id(0), key_dim, body)
    for r in range(CONV_WIDTH - 2):
        ns_ref[:, r, :] = st_ref[:, r + 1, :]
    ns_ref[:, CONV_WIDTH - 2, :] = x_ref[...]


def _conv_sample(states, layer, x, conv_w, key_dim, new_states):
    b, c = x.shape
    st_spec = pl.BlockSpec((None, b, CONV_WIDTH - 1, CONV_LANES), lambda l: (layer, 0, 0, l))
    ns_shape, extra, extra_specs, alias = _stacked_out(new_states, layer, states.shape, states.dtype)
    return pl.pallas_call(
        functools.partial(_conv_sample_kernel, key_dim=key_dim),
        grid=(c // CONV_LANES,),
        in_specs=[st_spec,
                  pl.BlockSpec((b, CONV_LANES), lambda l: (0, l)),
                  pl.BlockSpec((CONV_WIDTH, CONV_LANES), lambda l: (0, l))] + extra_specs,
        out_specs=[pl.BlockSpec((b, CONV_LANES), lambda l: (0, l)), st_spec],
        out_shape=[jax.ShapeDtypeStruct((b, c), f32), ns_shape],
        input_output_aliases=alias(3, 1),
        compiler_params=_params(("parallel",)),
        name="conv_sample",
    )(states, x, conv_w, *extra)


def _gate_kernel(ba_ref, a_ref, dt_ref, beta_ref, gcb_ref, gc_ref, *, rows, n_heads, cumulative):
    x = ba_ref[...]
    beta = jax.nn.sigmoid(x)
    xa = x + dt_ref[...]
    softplus = jnp.maximum(xa, 0.0) + jnp.log1p(jnp.exp(-jnp.abs(xa)))
    g = -jnp.exp(a_ref[...]) * softplus
    if cumulative:
        row = lax.broadcasted_iota(jnp.int32, (rows, 1), 0)
        g = jnp.where(row < jnp.where(pl.program_id(1) == 0, FRONT, 0), 0.0, g)
        shift = 1
        while shift < rows:
            g = g + jnp.where(row >= shift, pltpu.roll(g, shift, 0), 0.0)
            shift *= 2
    gc_ref[...] = g
    for h in range(n_heads):
        sl = slice(h * HEAD_DIM, (h + 1) * HEAD_DIM)
        beta_ref[:, sl] = jnp.broadcast_to(beta[:, h:h + 1], (rows, HEAD_DIM))
        gcb_ref[:, sl] = jnp.broadcast_to(g[:, n_heads + h:n_heads + h + 1], (rows, HEAD_DIM))


def _gates(ba3, a_log, dt_bias, cumulative):
    b, t, w = ba3.shape
    n_heads = w // 2
    rows = CHUNK if cumulative else t
    zeros = jnp.zeros((n_heads,), f32)
    a_pad = jnp.concatenate([zeros, a_log.astype(f32)]).reshape(1, w)
    dt_pad = jnp.concatenate([zeros, dt_bias.astype(f32)]).reshape(1, w)
    wide = n_heads * HEAD_DIM
    return pl.pallas_call(
        functools.partial(_gate_kernel, rows=rows, n_heads=n_heads, cumulative=cumulative),
        grid=(b, t // rows),
        in_specs=[pl.BlockSpec((None, rows, w), lambda i, j: (i, j, 0)),
                  pl.BlockSpec((1, w), lambda i, j: (0, 0)),
                  pl.BlockSpec((1, w), lambda i, j: (0, 0))],
        out_specs=[pl.BlockSpec((None, rows, wide), lambda i, j: (i, j, 0)),
                   pl.BlockSpec((None, rows, wide), lambda i, j: (i, j, 0)),
                   pl.BlockSpec((None, rows, w), lambda i, j: (i, j, 0))],
        out_shape=[jax.ShapeDtypeStruct((b, t, wide), f32),
                   jax.ShapeDtypeStruct((b, t, wide), f32),
                   jax.ShapeDtypeStruct((b, t, w), f32)],
        compiler_params=_params(("parallel", "parallel")),
        name="gdn_gates",
    )(ba3, a_pad, dt_pad)


def _dot_nt(a, b):
    return lax.dot_general(a, b, (((1,), (1,)), ((), ())), preferred_element_type=f32)


def _dot_tn(a, b):
    return lax.dot_general(a, b, (((0,), (0,)), ((), ())), preferred_element_type=f32)


def _dot16(a, b):
    return jnp.dot(a.astype(bf16), b.astype(bf16), preferred_element_type=f32)


def _dot_split(a, b):
    a_hi = a.astype(bf16)
    a_lo = (a - a_hi.astype(f32)).astype(bf16)
    b_hi = b.astype(bf16)
    b_lo = (b - b_hi.astype(f32)).astype(bf16)
    return (jnp.dot(a_hi, b_hi, preferred_element_type=f32)
            + (jnp.dot(a_hi, b_lo, preferred_element_type=f32)
               + jnp.dot(a_lo, b_hi, preferred_element_type=f32)))


def _stack(x, idx):
    return jnp.concatenate([x[:, i * HEAD_DIM:(i + 1) * HEAD_DIM] for i in idx], axis=0)


def _gdn_chunk_kernel(q_ref, k_ref, v_ref, beta_ref, gcb_ref, gcr_ref, z_ref, ng_ref,
                      o_ref, sout_ref, s_ref):
    c = pl.program_id(2)

    @pl.when(c == 0)
    def _():
        s_ref[...] = jnp.zeros_like(s_ref)

    quads = range(QUADS_PER_STEP)
    heads = tuple(range(QUAD))
    khead = tuple(h // 2 for h in heads)
    rows = QUAD * CHUNK
    kq_w = QUAD // 2 * HEAD_DIM
    v_w = QUAD * HEAD_DIM

    def head_rows(h):
        return slice(h * CHUNK, (h + 1) * CHUNK)

    def head_lanes(h):
        return slice(h * HEAD_DIM, (h + 1) * HEAD_DIM)

    ri = lax.broadcasted_iota(jnp.int32, (rows, rows), 0)
    ci = lax.broadcasted_iota(jnp.int32, (rows, rows), 1)
    shift = CHUNK.bit_length() - 1
    same = lax.shift_right_logical(ri, shift) == lax.shift_right_logical(ci, shift)
    lower = same & (ri >= ci)
    strict = ri > ci

    k16, kb16, q16, rhs, qd, kd16, decay, g_last = [], [], [], [], [], [], [], []
    for i in quads:
        kq = slice(i * kq_w, (i + 1) * kq_w)
        vs = slice(i * v_w, (i + 1) * v_w)
        gcb = gcb_ref[:, vs]
        kst = _stack(k_ref[:, kq], khead)
        qst = _stack(q_ref[:, kq], khead)
        bst = _stack(beta_ref[:, vs], heads)
        gst = _stack(gcb, heads)
        last = [gcb[CHUNK - 1:CHUNK, head_lanes(h)] for h in heads]
        glast = jnp.concatenate([jnp.broadcast_to(r, (CHUNK, HEAD_DIM)) for r in last], axis=0)
        eg = jnp.exp(gst)
        kb = kst * bst
        k16.append(kst.astype(bf16))
        kb16.append(kb.astype(bf16))
        q16.append(qst.astype(bf16))
        rhs.append(jnp.concatenate([_stack(v_ref[:, vs], heads) * bst, kb * eg], axis=1))
        qd.append(qst * eg)
        kd16.append((kst * jnp.exp(glast - gst)).astype(bf16))
        diff = jnp.concatenate([gst] * (rows // HEAD_DIM), axis=1) - gcr_ref[i]
        decay.append(jnp.exp(jnp.where(lower, diff, -jnp.inf)))
        g_last.append([jnp.exp(r) for r in last])

    pw = [-jnp.where(strict, _dot_nt(kb16[i], k16[i]) * decay[i], 0.0) for i in quads]
    attn16 = [(_dot_nt(q16[i], k16[i]) * decay[i]).astype(bf16) for i in quads]

    sol = [rhs[i] + _dot_split(pw[i], rhs[i]) for i in quads]
    for _ in range(NEUMANN_LEVELS - 1):
        pw16 = [p.astype(bf16) for p in pw]
        pw = [jnp.dot(p, p, preferred_element_type=f32) for p in pw16]
        sol = [sol[i] + _dot16(pw[i], sol[i]) for i in quads]

    x = [[_dot16(jnp.concatenate([sol[i][head_rows(h), HEAD_DIM:], qd[i][head_rows(h)]], axis=0),
                 s_ref[i * QUAD + h]) for h in heads] for i in quads]
    vn16 = [jnp.concatenate([sol[i][head_rows(h), :HEAD_DIM] - x[i][h][:CHUNK] for h in heads],
                            axis=0).astype(bf16) for i in quads]
    o = [jnp.concatenate([x[i][h][CHUNK:] for h in heads], axis=0)
         + jnp.dot(attn16[i], vn16[i], preferred_element_type=f32) for i in quads]
    for i in quads:
        for h in heads:
            s_ref[i * QUAD + h] = (s_ref[i * QUAD + h] * g_last[i][h]
                                   + _dot_tn(kd16[i][head_rows(h)], vn16[i][head_rows(h)]))

    for i in quads:
        on = o[i] * lax.rsqrt(jnp.mean(o[i] * o[i], axis=-1, keepdims=True) + EPS) * ng_ref[...]
        for h in heads:
            sl = slice(i * v_w + h * HEAD_DIM, i * v_w + (h + 1) * HEAD_DIM)
            o_ref[:, sl] = (on[head_rows(h)] * _silu(z_ref[:, sl])).astype(o_ref.dtype)

    @pl.when(c == pl.num_programs(2) - 1)
    def _():
        sout_ref[...] = s_ref[...]


def _gdn_chunked(qkv, beta_b, gc_b, gc_rows, z, norm_g, batch):
    m, value_dim = z.shape
    key_dim = (qkv.shape[1] - value_dim) // 2
    n_heads = value_dim // HEAD_DIM
    n_chunks = m // batch // CHUNK
    heads_per_step = QUAD * QUADS_PER_STEP
    n_groups = n_heads // heads_per_step
    kq_w = heads_per_step // 2 * HEAD_DIM
    v_w = heads_per_step * HEAD_DIM

    def row_spec(width, col0=0):
        off = col0 // width
        return pl.BlockSpec((CHUNK, width), lambda b, g, c: (b * n_chunks + c, g + off))

    return pl.pallas_call(
        _gdn_chunk_kernel,
        grid=(batch, n_groups, n_chunks),
        in_specs=[row_spec(kq_w), row_spec(kq_w, key_dim), row_spec(v_w, 2 * key_dim),
                  row_spec(v_w), row_spec(v_w),
                  pl.BlockSpec((None, None, QUADS_PER_STEP, 1, QUAD * CHUNK),
                               lambda b, g, c: (b, c, g, 0, 0)),
                  row_spec(v_w),
                  pl.BlockSpec((1, HEAD_DIM), lambda b, g, c: (0, 0))],
        out_specs=[row_spec(v_w),
                   pl.BlockSpec((None, heads_per_step, HEAD_DIM, HEAD_DIM),
                                lambda b, g, c: (b, g, 0, 0))],
        out_shape=[jax.ShapeDtypeStruct((m, value_dim), bf16),
                   jax.ShapeDtypeStruct((batch, n_heads, HEAD_DIM, HEAD_DIM), f32)],
        scratch_shapes=[pltpu.VMEM((heads_per_step, HEAD_DIM, HEAD_DIM), f32)],
        compiler_params=_params(("parallel", "parallel", "arbitrary")),
        name="gdn_chunked",
    )(qkv, qkv, qkv, beta_b, gc_b, gc_rows, z, norm_g.reshape(1, HEAD_DIM))


def _gdn_step_kernel(*refs, n_heads):
    qt_ref, kt_ref, v_ref, beta_ref, g_ref, z_ref, ng_ref, s_ref = refs[:8]
    o_ref, sout_ref, kcol_ref, qcol_ref, eg_ref, oraw_ref = refs[-6:]
    n_k = qt_ref.shape[1]
    rep = n_heads // n_k
    for kh in range(n_k):
        kcol_ref[kh] = jnp.broadcast_to(kt_ref[:, kh:kh + 1], (HEAD_DIM, HEAD_DIM))
        qcol_ref[kh] = jnp.broadcast_to(qt_ref[:, kh:kh + 1], (HEAD_DIM, HEAD_DIM))
    eg_ref[...] = jnp.exp(g_ref[...])
    for h in range(n_heads):
        kcol = kcol_ref[h // rep]
        s = s_ref[h] * eg_ref[h:h + 1, :]
        kv = jnp.sum(kcol * s, axis=0, keepdims=True)
        delta = (v_ref[h:h + 1, :] - kv) * beta_ref[h:h + 1, :]
        s = s + kcol * delta
        sout_ref[h] = s
        oraw_ref[h:h + 1, :] = jnp.sum(qcol_ref[h // rep] * s, axis=0, keepdims=True)
    o = oraw_ref[...]
    on = o * lax.rsqrt(jnp.mean(o * o, axis=-1, keepdims=True) + EPS) * ng_ref[...]
    o_ref[...] = on * _silu(z_ref[...])


def _gdn_step(qt, kt, v3, beta3, g3, z3, norm_g, states, layer, new_states):
    b, n_heads, _ = v3.shape
    n_k = qt.shape[2]

    def head_spec():
        return pl.BlockSpec((None, n_heads, HEAD_DIM), lambda i: (i, 0, 0))

    def t_spec():
        return pl.BlockSpec((None, HEAD_DIM, n_k), lambda i: (i, 0, 0))

    s_spec = pl.BlockSpec((None, None, n_heads, HEAD_DIM, HEAD_DIM),
                          lambda i: (layer, i, 0, 0, 0))
    ns_shape, extra, extra_specs, alias = _stacked_out(new_states, layer, states.shape, f32)
    return pl.pallas_call(
        functools.partial(_gdn_step_kernel, n_heads=n_heads),
        grid=(b,),
        in_specs=[t_spec(), t_spec(), head_spec(), head_spec(), head_spec(), head_spec(),
                  pl.BlockSpec((1, HEAD_DIM), lambda i: (0, 0)), s_spec] + extra_specs,
        out_specs=[head_spec(), s_spec],
        out_shape=[jax.ShapeDtypeStruct((b, n_heads, HEAD_DIM), f32), ns_shape],
        scratch_shapes=[pltpu.VMEM((n_k, HEAD_DIM, HEAD_DIM), f32),
                        pltpu.VMEM((n_k, HEAD_DIM, HEAD_DIM), f32),
                        pltpu.VMEM((n_heads, HEAD_DIM), f32),
                        pltpu.VMEM((n_heads, HEAD_DIM), f32)],
        input_output_aliases=alias(8, 1),
        compiler_params=_params(("parallel",)),
        name="gdn_step",
    )(qt, kt, v3, beta3, g3, z3, norm_g.reshape(1, HEAD_DIM), states, *extra)


def _pool_layer(hp, hs, states, layer, new_states, w_in, w_grp, scale, w_out, norm_g, batch):
    e = w_grp.shape[1] * w_grp.shape[2]

    def project(h):
        xn = _rmsnorm(h, norm_g, bf16)
        return _matmul(xn, w_in, layer, 0, e), _matmul(xn, w_in, layer, e, e)

    def mix(h, d, z):
        return _out_project(h, _grouped_matmul_gate(d, w_grp, layer, scale, z), w_out, layer)

    u, z = project(hp)
    u3 = u.reshape(batch, -1, e)
    hp = mix(hp, _pool_diff_prompt(u3).reshape(u.shape), z)
    pool_p = u3[:, -POOL_CTX:]

    u, z = project(hs)
    d, new_states = _pool_diff_sample(states, layer, u, new_states)
    hs = mix(hs, d, z)
    return hp, hs, pool_p, new_states


def _gdn_layer(hp, hs, conv_states, ssm_states, layer, new_conv, new_ssm, w_in, conv_w, a_log,
               dt_bias, head_g, w_out, norm_g, batch):
    value_dim = w_out.shape[1]
    n_heads = value_dim // HEAD_DIM
    conv_dim = conv_w.shape[1]
    key_dim = (conv_dim - value_dim) // 2
    n_k = key_dim // HEAD_DIM

    def project(h):
        xn = _rmsnorm(h, norm_g, bf16)
        qkv = _matmul(xn, w_in, layer, 0, conv_dim)
        z = _matmul(xn, w_in, layer, conv_dim, value_dim)
        ba = _matmul(xn, w_in, layer, conv_dim + value_dim, 2 * n_heads)
        return qkv, z, ba

    qkv, z, ba = project(hp)
    t = hp.shape[0] // batch
    qkv3 = qkv.reshape(batch, t, conv_dim)
    act = _conv_prompt(qkv3, conv_w, key_dim).reshape(-1, conv_dim)
    beta_b, gc_b, gc = _gates(ba.reshape(batch, t, 2 * n_heads), a_log, dt_bias, True)
    gc_rows = gc[..., n_heads:].reshape(batch, t // CHUNK, CHUNK, n_heads // QUAD, QUAD)
    gc_rows = jnp.transpose(gc_rows, (0, 1, 3, 4, 2)).reshape(
        batch, t // CHUNK, n_heads // QUAD, 1, QUAD * CHUNK)
    o, ssm_p = _gdn_chunked(act, beta_b.reshape(-1, value_dim), gc_b.reshape(-1, value_dim),
                            gc_rows, z, head_g, batch)
    hp = _out_project(hp, o, w_out, layer)
    conv_p = qkv3[:, -(CONV_WIDTH - 1):]

    qkv, z, ba = project(hs)
    bs = hs.shape[0]
    act, new_conv = _conv_sample(conv_states, layer, qkv, conv_w, key_dim, new_conv)
    beta_b, g_b, _ = _gates(ba.reshape(1, bs, 2 * n_heads), a_log, dt_bias, False)
    qt = jnp.transpose(act[:, :key_dim].reshape(bs, n_k, HEAD_DIM), (0, 2, 1))
    kt = jnp.transpose(act[:, key_dim:2 * key_dim].reshape(bs, n_k, HEAD_DIM), (0, 2, 1))
    per_head = (bs, n_heads, HEAD_DIM)
    o, new_ssm = _gdn_step(qt, kt, act[:, 2 * key_dim:].reshape(per_head),
                           beta_b.reshape(per_head), g_b.reshape(per_head), z.reshape(per_head),
                           head_g, ssm_states, layer, new_ssm)
    hs = _out_project(hs, o.reshape(bs, value_dim).astype(bf16), w_out, layer)
    return hp, hs, conv_p, ssm_p, new_conv, new_ssm


def kernel(x_prompt, x_sample, state_pool, state_conv, state_ssm, meta_tokens, norm_g, final_norm_g, pool_w_in, pool_w_grp, pool_scale, pool_w_out, gdn_w_in, gdn_conv_w, gdn_A_log, gdn_dt_bias, gdn_norm_g, gdn_w_out):
    dt = x_prompt.dtype
    batch, seq, d = x_prompt.shape
    depth = norm_g.shape[0]
    meta = jnp.broadcast_to(meta_tokens.astype(dt)[None], (batch, N_META, d))
    hp = jnp.concatenate([jnp.zeros((batch, FRONT, d), dt), meta, x_prompt], axis=1)
    t = hp.shape[1]
    hp = hp.reshape(batch * t, d)
    hs = x_sample.reshape(x_sample.shape[0], d)
    pool_p, conv_p, ssm_p = [], [], []
    pool_s = conv_s = ssm_s = None
    for i in range(depth):
        j = i // 2
        if i % 2 == 0:
            hp, hs, pp, pool_s = _pool_layer(hp, hs, state_pool, j, pool_s, pool_w_in, pool_w_grp,
                                             pool_scale[j], pool_w_out, norm_g[i], batch)
            pool_p.append(pp)
        else:
            hp, hs, cp, sp, conv_s, ssm_s = _gdn_layer(
                hp, hs, state_conv, state_ssm, j, conv_s, ssm_s, gdn_w_in, gdn_conv_w[j],
                gdn_A_log[j], gdn_dt_bias[j], gdn_norm_g[j], gdn_w_out, norm_g[i], batch)
            conv_p.append(cp)
            ssm_p.append(sp)
    y_prompt = _final_norm_prompt(hp.reshape(batch, t, d), final_norm_g, seq)
    y_sample = _rmsnorm(hs, final_norm_g, dt).reshape(x_sample.shape)
    return (y_prompt, y_sample, jnp.stack(pool_p), jnp.stack(conv_p), jnp.stack(ssm_p),
            pool_s, conv_s, ssm_s)
```

```python
import functools

import jax
import jax.numpy as jnp
from jax import lax
from jax.experimental import pallas as pl
from jax.experimental.pallas import tpu as pltpu

N_META = 16
PAST_LEN = 16384
POOL_WINDOWS = (2, 4, 8, 16)
POOL_CTX = max(POOL_WINDOWS) - 1
HEAD_DIM = 128
CONV_WIDTH = 4
CHUNK = 64
EPS = 1e-6
FRONT = (-N_META) % CHUNK
QUAD = 4
QUADS_PER_STEP = 8
NEUMANN_LEVELS = 6

VMEM_LIMIT = 56 * 1024 * 1024
SUBLANES = 8

f32 = jnp.float32
bf16 = jnp.bfloat16


def _params(sem):
    return pltpu.CompilerParams(dimension_semantics=sem, vmem_limit_bytes=VMEM_LIMIT)


def _silu(x):
    return x * jax.nn.sigmoid(x)


def _stacked_out(stack, layer, shape, dtype):
    if stack is None:
        return jax.ShapeDtypeStruct(shape, dtype), [], [], lambda n_in, n_out: {}
    return (jax.ShapeDtypeStruct(stack.shape, stack.dtype), [stack],
            [pl.BlockSpec(memory_space=pl.ANY)], lambda n_in, n_out: {n_in: n_out})


def _rmsnorm_kernel(x_ref, g_ref, o_ref):
    x = x_ref[...]
    y = x * lax.rsqrt(jnp.mean(x * x, axis=-1, keepdims=True) + EPS)
    o_ref[...] = (y * g_ref[...]).astype(o_ref.dtype)


def _rmsnorm(x, g, out_dtype):
    m, d = x.shape
    tm = 256 if m % 256 == 0 else m
    return pl.pallas_call(
        _rmsnorm_kernel,
        grid=(m // tm,),
        in_specs=[pl.BlockSpec((tm, d), lambda i: (i, 0)),
                  pl.BlockSpec((1, d), lambda i: (0, 0))],
        out_specs=pl.BlockSpec((tm, d), lambda i: (i, 0)),
        out_shape=jax.ShapeDtypeStruct((m, d), out_dtype),
        compiler_params=_params(("parallel",)),
        name="rmsnorm",
    )(x, g.reshape(1, d))


def _final_norm_prompt(h3, g, seq):
    b, t, d = h3.shape
    skip = (t - seq) // CHUNK
    return pl.pallas_call(
        _rmsnorm_kernel,
        grid=(b, seq // CHUNK),
        in_specs=[pl.BlockSpec((None, CHUNK, d), lambda i, j: (i, j + skip, 0)),
                  pl.BlockSpec((1, d), lambda i, j: (0, 0))],
        out_specs=pl.BlockSpec((None, CHUNK, d), lambda i, j: (i, j, 0)),
        out_shape=jax.ShapeDtypeStruct((b, seq, d), h3.dtype),
        compiler_params=_params(("parallel", "parallel")),
        name="final_norm",
    )(h3, g.reshape(1, d))


MM_K = 4096
MM_TN = 512


def _mm_kernel(*refs, epilogue, m_axis):
    x_ref, w_ref = refs[0], refs[1]
    o_ref, w16_ref = refs[-2], refs[-1]

    @pl.when(pl.program_id(m_axis) == 0)
    def _():
        w16_ref[...] = w_ref[...].astype(bf16)

    acc = jnp.dot(x_ref[...], w16_ref[...], preferred_element_type=f32)
    if epilogue == "plain":
        o_ref[...] = acc.astype(o_ref.dtype)
    elif epilogue == "gate":
        scale_ref, z_ref = refs[2], refs[3]
        o_ref[...] = (acc * scale_ref[...] * _silu(z_ref[...])).astype(o_ref.dtype)
    elif epilogue == "residual":
        h_ref = refs[2]
        o_ref[...] = h_ref[...] + acc
    else:
        raise ValueError(epilogue)


def _mm_rows(m, epilogue):
    for tm in ((1056,) if epilogue == "residual" else (1408, 1056)):
        if m % tm == 0:
            return tm
    return m


def _matmul(x, w, layer, col0, ncols, kblock=0, out_dtype=f32, residual=None):
    m = x.shape[0]
    tk = min(MM_K, x.shape[1])
    epilogue = "plain" if residual is None else "residual"
    tm = _mm_rows(m, epilogue)
    tn = min(MM_TN, ncols)
    off = col0 // tn
    assert col0 % tn == 0 and ncols % tn == 0 and x.shape[1] % tk == 0
    in_specs = [pl.BlockSpec((tm, tk), lambda j, i: (i, kblock)),
                pl.BlockSpec((None, tk, tn), lambda j, i: (layer, kblock, j + off))]
    args = [x, w]
    if residual is not None:
        in_specs.append(pl.BlockSpec((tm, tn), lambda j, i: (i, j)))
        args.append(residual)
    return pl.pallas_call(
        functools.partial(_mm_kernel, epilogue=epilogue, m_axis=1),
        grid=(ncols // tn, m // tm),
        in_specs=in_specs,
        out_specs=pl.BlockSpec((tm, tn), lambda j, i: (i, j)),
        out_shape=jax.ShapeDtypeStruct((m, ncols), out_dtype),
        scratch_shapes=[pltpu.VMEM((tk, tn), bf16)],
        compiler_params=_params(("parallel", "arbitrary")),
        name="matmul",
    )(*args)


def _out_project(h, act, w_out, layer):
    tk = min(MM_K, act.shape[1])
    for kb in range(act.shape[1] // tk):
        h = _matmul(act, w_out, layer, 0, w_out.shape[2], kblock=kb, residual=h)
    return h


def _grouped_matmul_gate(d, w_grp, layer, scale, z):
    m, e = d.shape
    _, ng, gk, gn = w_grp.shape
    tm = _mm_rows(m, "gate")
    tn = MM_TN
    nb = gn // tn
    return pl.pallas_call(
        functools.partial(_mm_kernel, epilogue="gate", m_axis=2),
        grid=(ng, nb, m // tm),
        in_specs=[pl.BlockSpec((tm, gk), lambda g, j, i: (i, g)),
                  pl.BlockSpec((None, None, gk, tn), lambda g, j, i: (layer, g, 0, j)),
                  pl.BlockSpec((1, tn), lambda g, j, i: (0, g * nb + j)),
                  pl.BlockSpec((tm, tn), lambda g, j, i: (i, g * nb + j))],
        out_specs=pl.BlockSpec((tm, tn), lambda g, j, i: (i, g * nb + j)),
        out_shape=jax.ShapeDtypeStruct((m, e), bf16),
        scratch_shapes=[pltpu.VMEM((gk, tn), bf16)],
        compiler_params=_params(("parallel", "parallel", "arbitrary")),
        name="grouped_matmul_gate",
    )(d, w_grp, scale.reshape(1, e), z)


POOL_HALO = 16
POOL_ROWS = 16
POOL_LANES = 512


def _row_pieces(x):
    return [x[r:r + SUBLANES] for r in range(0, x.shape[0], SUBLANES)]


def _shift_rows(pieces, d):
    if d == SUBLANES:
        return [pieces[0]] + pieces[:-1]
    rolled = [pltpu.roll(p, d, 0) for p in pieces]
    keep = lax.broadcasted_iota(jnp.int32, pieces[0].shape, 0) >= d
    return [rolled[0]] + [jnp.where(keep, rolled[k], rolled[k - 1]) for k in range(1, len(pieces))]


def _window_sums(pieces, w):
    d = 1
    while d < w:
        pieces = [a + b for a, b in zip(pieces, _shift_rows(pieces, d))]
        d *= 2
    return pieces


def _pool_diff_prompt_kernel(halo_ref, cur_ref, o_ref, *, tm, gsz):
    t = pl.program_id(1)
    gi = (pl.program_id(2) * POOL_LANES) // gsz
    for widx, w in enumerate(POOL_WINDOWS):
        ctx = -(-(w - 1) // SUBLANES) * SUBLANES

        @pl.when(gi == widx)
        def _(w=w, ctx=ctx):
            for r in range(0, tm, POOL_ROWS):
                cur = cur_ref[r:r + POOL_ROWS, :]
                if r == 0:
                    prev = jnp.where(t > 0, halo_ref[POOL_HALO - ctx:POOL_HALO, :], 0.0)
                else:
                    prev = cur_ref[r - ctx:r, :]
                s = _window_sums(_row_pieces(prev) + _row_pieces(cur), w)[ctx // SUBLANES:]
                s = jnp.concatenate(s, axis=0)
                pos = lax.broadcasted_iota(jnp.int32, (POOL_ROWS, 1), 0) + (t * tm + r - FRONT)
                cnt = jnp.clip(pos + 1, 1, w).astype(f32)
                o_ref[r:r + POOL_ROWS, :] = (s / cnt - cur).astype(o_ref.dtype)


def _pool_diff_prompt(u3):
    b, t, e = u3.shape
    tm = 1056
    assert t % tm == 0 and tm % POOL_HALO == 0 and e % (len(POOL_WINDOWS) * POOL_LANES) == 0
    hb = tm // POOL_HALO
    return pl.pallas_call(
        functools.partial(_pool_diff_prompt_kernel, tm=tm, gsz=e // len(POOL_WINDOWS)),
        grid=(b, t // tm, e // POOL_LANES),
        in_specs=[pl.BlockSpec((None, POOL_HALO, POOL_LANES),
                               lambda i, j, c: (i, jnp.maximum(j * hb - 1, 0), c)),
                  pl.BlockSpec((None, tm, POOL_LANES), lambda i, j, c: (i, j, c))],
        out_specs=pl.BlockSpec((None, tm, POOL_LANES), lambda i, j, c: (i, j, c)),
        out_shape=jax.ShapeDtypeStruct((b, t, e), bf16),
        compiler_params=_params(("parallel", "parallel", "parallel")),
        name="pool_diff_prompt",
    )(u3, u3)


def _pool_diff_sample_kernel(*refs, gsz, lanes):
    st_ref, u_ref = refs[0], refs[1]
    o_ref, ns_ref = refs[-2], refs[-1]
    gi = (pl.program_id(1) * lanes) // gsz
    for widx, w in enumerate(POOL_WINDOWS):
        @pl.when(gi == widx)
        def _(w=w):
            cur = u_ref[...]
            s = cur
            for i in range(1, w):
                s = s + st_ref[:, POOL_CTX - i, :]
            cnt = float(min(PAST_LEN + 1, w))
            o_ref[...] = (s / cnt - cur).astype(o_ref.dtype)
    for r in range(POOL_CTX - 1):
        ns_ref[:, r, :] = st_ref[:, r + 1, :]
    ns_ref[:, POOL_CTX - 1, :] = u_ref[...]


def _pool_diff_sample(states, layer, u, new_states):
    _, b, _, e = states.shape
    gsz = e // len(POOL_WINDOWS)
    bb, lanes = 16, min(1024, gsz)
    assert b % bb == 0 and gsz % lanes == 0
    st_spec = pl.BlockSpec((None, bb, POOL_CTX, lanes), lambda i, c: (layer, i, 0, c))
    ns_shape, extra, extra_specs, alias = _stacked_out(new_states, layer, states.shape, states.dtype)
    return pl.pallas_call(
        functools.partial(_pool_diff_sample_kernel, gsz=gsz, lanes=lanes),
        grid=(b // bb, e // lanes),
        in_specs=[st_spec, pl.BlockSpec((bb, lanes), lambda i, c: (i, c))] + extra_specs,
        out_specs=[pl.BlockSpec((bb, lanes), lambda i, c: (i, c)), st_spec],
        out_shape=[jax.ShapeDtypeStruct((b, e), bf16), ns_shape],
        input_output_aliases=alias(2, 1),
        compiler_params=_params(("parallel", "parallel")),
        name="pool_diff_sample",
    )(states, u, *extra)


CONV_HALO = 8
CONV_ROWS = 16
CONV_LANES = 1024


def _conv_act(acc, scale):
    y = _silu(acc)
    if scale is None:
        return y
    parts = []
    for h in range(y.shape[1] // HEAD_DIM):
        yh = y[:, h * HEAD_DIM:(h + 1) * HEAD_DIM]
        parts.append(yh * lax.rsqrt(jnp.sum(yh * yh, axis=-1, keepdims=True) + EPS) * scale)
    return jnp.concatenate(parts, axis=1)


def _conv_branches(c, key_dim, body):
    normalise = c * CONV_LANES < 2 * key_dim
    q_scale = jnp.where(c * CONV_LANES < key_dim, HEAD_DIM ** -0.5, 1.0)
    pl.when(normalise)(lambda: body(q_scale))
    pl.when(jnp.logical_not(normalise))(lambda: body(None))


def _conv_prompt_kernel(halo_ref, cur_ref, w_ref, o_ref, *, tm, key_dim):
    t = pl.program_id(1)

    def body(scale):
        for r in range(0, tm, CONV_ROWS):
            if r == 0:
                prev = jnp.where(t > 0, halo_ref[...], 0.0)
            else:
                prev = cur_ref[r - CONV_HALO:r, :]
            cur = cur_ref[r:r + CONV_ROWS, :]
            pieces = _row_pieces(prev) + _row_pieces(cur)
            acc = cur * w_ref[CONV_WIDTH - 1:CONV_WIDTH, :]
            for d in range(1, CONV_WIDTH):
                tap = CONV_WIDTH - 1 - d
                back = jnp.concatenate(_shift_rows(pieces, d)[CONV_HALO // SUBLANES:], axis=0)
                acc = acc + back * w_ref[tap:tap + 1, :]
            o_ref[r:r + CONV_ROWS, :] = _conv_act(acc, scale)

    _conv_branches(pl.program_id(2), key_dim, body)


def _conv_prompt(x3, conv_w, key_dim):
    b, t, c = x3.shape
    tm = 1056
    assert t % tm == 0 and tm % CONV_ROWS == 0 and tm % CONV_HALO == 0 and key_dim % CONV_LANES == 0
    hb = tm // CONV_HALO
    return pl.pallas_call(
        functools.partial(_conv_prompt_kernel, tm=tm, key_dim=key_dim),
        grid=(b, t // tm, c // CONV_LANES),
        in_specs=[pl.BlockSpec((None, CONV_HALO, CONV_LANES),
                               lambda i, j, l: (i, jnp.maximum(j * hb - 1, 0), l)),
                  pl.BlockSpec((None, tm, CONV_LANES), lambda i, j, l: (i, j, l)),
                  pl.BlockSpec((CONV_WIDTH, CONV_LANES), lambda i, j, l: (0, l))],
        out_specs=pl.BlockSpec((None, tm, CONV_LANES), lambda i, j, l: (i, j, l)),
        out_shape=jax.ShapeDtypeStruct((b, t, c), f32),
        compiler_params=_params(("parallel", "parallel", "parallel")),
        name="conv_prompt",
    )(x3, x3, conv_w)


def _conv_sample_kernel(*refs, key_dim):
    st_ref, x_ref, w_ref = refs[0], refs[1], refs[2]
    o_ref, ns_ref = refs[-2], refs[-1]

    def body(scale):
        acc = x_ref[...] * w_ref[CONV_WIDTH - 1:CONV_WIDTH, :]
        for i in range(CONV_WIDTH - 1):
            acc = acc + st_ref[:, i, :] * w_ref[i:i + 1, :]
        o_ref[...] = _conv_act(acc, scale)

    _conv_branches(pl.program_id(0), key_dim, body)
    for r in range(CONV_WIDTH - 2):
        ns_ref[:, r, :] = st_ref[:, r + 1, :]
    ns_ref[:, CONV_WIDTH - 2, :] = x_ref[...]


def _conv_sample(states, layer, x, conv_w, key_dim, new_states):
    b, c = x.shape
    st_spec = pl.BlockSpec((None, b, CONV_WIDTH - 1, CONV_LANES), lambda l: (layer, 0, 0, l))
    ns_shape, extra, extra_specs, alias = _stacked_out(new_states, layer, states.shape, states.dtype)
    return pl.pallas_call(
        functools.partial(_conv_sample_kernel, key_dim=key_dim),
        grid=(c // CONV_LANES,),
        in_specs=[st_spec,
                  pl.BlockSpec((b, CONV_LANES), lambda l: (0, l)),
                  pl.BlockSpec((CONV_WIDTH, CONV_LANES), lambda l: (0, l))] + extra_specs,
        out_specs=[pl.BlockSpec((b, CONV_LANES), lambda l: (0, l)), st_spec],
        out_shape=[jax.ShapeDtypeStruct((b, c), f32), ns_shape],
        input_output_aliases=alias(3, 1),
        compiler_params=_params(("parallel",)),
        name="conv_sample",
    )(states, x, conv_w, *extra)


def _gate_values(ba_ref, a_ref, dt_ref):
    x = ba_ref[...]
    xa = x + dt_ref[...]
    softplus = jnp.maximum(xa, 0.0) + jnp.log1p(jnp.exp(-jnp.abs(xa)))
    return jax.nn.sigmoid(x), -jnp.exp(a_ref[...]) * softplus


def _gate_chunk_kernel(ba_ref, a_ref, dt_ref, o_ref, *, n_heads):
    beta, g = _gate_values(ba_ref, a_ref, dt_ref)
    row = lax.broadcasted_iota(jnp.int32, (CHUNK, 1), 0)
    g = jnp.where(row < jnp.where(pl.program_id(1) == 0, FRONT, 0), 0.0, g)
    shift = 1
    while shift < CHUNK:
        g = g + jnp.where(row >= shift, pltpu.roll(g, shift, 0), 0.0)
        shift *= 2
    lane = lax.broadcasted_iota(jnp.int32, g.shape, 1)
    o_ref[...] = jnp.where(lane < n_heads, beta, g)


def _gates_chunked(ba3, a_pad, dt_pad):
    b, t, w = ba3.shape
    return pl.pallas_call(
        functools.partial(_gate_chunk_kernel, n_heads=w // 2),
        grid=(b, t // CHUNK),
        in_specs=[pl.BlockSpec((None, CHUNK, w), lambda i, j: (i, j, 0)),
                  pl.BlockSpec((1, w), lambda i, j: (0, 0)),
                  pl.BlockSpec((1, w), lambda i, j: (0, 0))],
        out_specs=pl.BlockSpec((None, CHUNK, w), lambda i, j: (i, j, 0)),
        out_shape=jax.ShapeDtypeStruct((b, t, w), f32),
        compiler_params=_params(("parallel", "parallel")),
        name="gdn_gates_chunked",
    )(ba3, a_pad, dt_pad)


def _gate_step_kernel(ba_ref, a_ref, dt_ref, beta_ref, g_ref, *, n_heads):
    beta, g = _gate_values(ba_ref, a_ref, dt_ref)
    rows = beta.shape[0]
    for h in range(n_heads):
        sl = slice(h * HEAD_DIM, (h + 1) * HEAD_DIM)
        beta_ref[:, sl] = jnp.broadcast_to(beta[:, h:h + 1], (rows, HEAD_DIM))
        g_ref[:, sl] = jnp.broadcast_to(g[:, n_heads + h:n_heads + h + 1], (rows, HEAD_DIM))


def _gates_step(ba, a_pad, dt_pad):
    b, w = ba.shape
    n_heads = w // 2
    wide = n_heads * HEAD_DIM
    return pl.pallas_call(
        functools.partial(_gate_step_kernel, n_heads=n_heads),
        out_shape=[jax.ShapeDtypeStruct((b, wide), f32), jax.ShapeDtypeStruct((b, wide), f32)],
        compiler_params=pltpu.CompilerParams(vmem_limit_bytes=VMEM_LIMIT),
        name="gdn_gates_step",
    )(ba, a_pad, dt_pad)


def _gate_params(a_log, dt_bias):
    zeros = jnp.zeros(a_log.shape, f32)
    w = 2 * a_log.shape[0]
    return (jnp.concatenate([zeros, a_log.astype(f32)]).reshape(1, w),
            jnp.concatenate([zeros, dt_bias.astype(f32)]).reshape(1, w))


def _dot_nt(a, b):
    return lax.dot_general(a, b, (((1,), (1,)), ((), ())), preferred_element_type=f32)


def _dot_tn(a, b):
    return lax.dot_general(a, b, (((0,), (0,)), ((), ())), preferred_element_type=f32)


def _dot16(a, b):
    return jnp.dot(a.astype(bf16), b.astype(bf16), preferred_element_type=f32)


def _dot_split(a, b):
    a_hi = a.astype(bf16)
    a_lo = (a - a_hi.astype(f32)).astype(bf16)
    b_hi = b.astype(bf16)
    b_lo = (b - b_hi.astype(f32)).astype(bf16)
    return (jnp.dot(a_hi, b_hi, preferred_element_type=f32)
            + (jnp.dot(a_hi, b_lo, preferred_element_type=f32)
               + jnp.dot(a_lo, b_hi, preferred_element_type=f32)))


def _stack(x, idx):
    return jnp.concatenate([x[:, i * HEAD_DIM:(i + 1) * HEAD_DIM] for i in idx], axis=0)


def _gdn_chunk_kernel(q_ref, k_ref, v_ref, bg_ref, gcr_ref, z_ref, ng_ref,
                      o_ref, sout_ref, s_ref):
    c = pl.program_id(2)

    @pl.when(c == 0)
    def _():
        s_ref[...] = jnp.zeros_like(s_ref)

    quads = range(QUADS_PER_STEP)
    heads = tuple(range(QUAD))
    khead = tuple(h // 2 for h in heads)
    rows = QUAD * CHUNK
    kq_w = QUAD // 2 * HEAD_DIM
    v_w = QUAD * HEAD_DIM

    def head_rows(h):
        return slice(h * CHUNK, (h + 1) * CHUNK)

    ri = lax.broadcasted_iota(jnp.int32, (rows, rows), 0)
    ci = lax.broadcasted_iota(jnp.int32, (rows, rows), 1)
    shift = CHUNK.bit_length() - 1
    same = lax.shift_right_logical(ri, shift) == lax.shift_right_logical(ci, shift)
    lower = same & (ri >= ci)
    strict = ri > ci

    bg = bg_ref[...]
    n_step = QUAD * QUADS_PER_STEP
    k16, kb16, q16, rhs, qd, kd16, decay, g_last = [], [], [], [], [], [], [], []
    for i in quads:
        kq = slice(i * kq_w, (i + 1) * kq_w)
        vs = slice(i * v_w, (i + 1) * v_w)
        kst = _stack(k_ref[:, kq], khead)
        qst = _stack(q_ref[:, kq], khead)
        cols = [i * QUAD + h for h in heads]
        bcol = [jnp.broadcast_to(bg[:, n:n + 1], (CHUNK, HEAD_DIM)) for n in cols]
        gcol = [jnp.broadcast_to(bg[:, n_step + n:n_step + n + 1], (CHUNK, HEAD_DIM)) for n in cols]
        bst = jnp.concatenate(bcol, axis=0)
        gst = jnp.concatenate(gcol, axis=0)
        last = [g[CHUNK - 1:CHUNK] for g in gcol]
        glast = jnp.concatenate([jnp.broadcast_to(r, (CHUNK, HEAD_DIM)) for r in last], axis=0)
        eg = jnp.exp(gst)
        kb = kst * bst
        k16.append(kst.astype(bf16))
        kb16.append(kb.astype(bf16))
        q16.append(qst.astype(bf16))
        rhs.append(jnp.concatenate([_stack(v_ref[:, vs], heads) * bst, kb * eg], axis=1))
        qd.append(qst * eg)
        kd16.append((kst * jnp.exp(glast - gst)).astype(bf16))
        diff = jnp.concatenate([gst] * (rows // HEAD_DIM), axis=1) - gcr_ref[i]
        decay.append(jnp.exp(jnp.where(lower, diff, -jnp.inf)))
        g_last.append([jnp.exp(r) for r in last])

    pw = [-jnp.where(strict, _dot_nt(kb16[i], k16[i]) * decay[i], 0.0) for i in quads]
    attn16 = [(_dot_nt(q16[i], k16[i]) * decay[i]).astype(bf16) for i in quads]

    sol = [rhs[i] + _dot_split(pw[i], rhs[i]) for i in quads]
    for _ in range(NEUMANN_LEVELS - 1):
        pw16 = [p.astype(bf16) for p in pw]
        pw = [jnp.dot(p, p, preferred_element_type=f32) for p in pw16]
        sol = [sol[i] + _dot16(pw[i], sol[i]) for i in quads]

    x = [[_dot16(jnp.concatenate([sol[i][head_rows(h), HEAD_DIM:], qd[i][head_rows(h)]], axis=0),
                 s_ref[i * QUAD + h]) for h in heads] for i in quads]
    vn16 = [jnp.concatenate([sol[i][head_rows(h), :HEAD_DIM] - x[i][h][:CHUNK] for h in heads],
                            axis=0).astype(bf16) for i in quads]
    o = [jnp.concatenate([x[i][h][CHUNK:] for h in heads], axis=0)
         + jnp.dot(attn16[i], vn16[i], preferred_element_type=f32) for i in quads]
    for i in quads:
        for h in heads:
            s_ref[i * QUAD + h] = (s_ref[i * QUAD + h] * g_last[i][h]
                                   + _dot_tn(kd16[i][head_rows(h)], vn16[i][head_rows(h)]))

    for i in quads:
        on = o[i] * lax.rsqrt(jnp.mean(o[i] * o[i], axis=-1, keepdims=True) + EPS) * ng_ref[...]
        for h in heads:
            sl = slice(i * v_w + h * HEAD_DIM, i * v_w + (h + 1) * HEAD_DIM)
            o_ref[:, sl] = (on[head_rows(h)] * _silu(z_ref[:, sl])).astype(o_ref.dtype)

    @pl.when(c == pl.num_programs(2) - 1)
    def _():
        sout_ref[...] = s_ref[...]


def _gdn_chunked(qkv, bg, gc_rows, z, norm_g, batch):
    m, value_dim = z.shape
    key_dim = (qkv.shape[1] - value_dim) // 2
    n_heads = value_dim // HEAD_DIM
    n_chunks = m // batch // CHUNK
    heads_per_step = QUAD * QUADS_PER_STEP
    n_groups = n_heads // heads_per_step
    kq_w = heads_per_step // 2 * HEAD_DIM
    v_w = heads_per_step * HEAD_DIM

    def row_spec(width, col0=0):
        off = col0 // width
        return pl.BlockSpec((CHUNK, width), lambda b, g, c: (b * n_chunks + c, g + off))

    return pl.pallas_call(
        _gdn_chunk_kernel,
        grid=(batch, n_groups, n_chunks),
        in_specs=[row_spec(kq_w), row_spec(kq_w, key_dim), row_spec(v_w, 2 * key_dim),
                  pl.BlockSpec((None, CHUNK, 2 * heads_per_step),
                               lambda b, g, c: (g, b * n_chunks + c, 0)),
                  pl.BlockSpec((None, None, QUADS_PER_STEP, 1, QUAD * CHUNK),
                               lambda b, g, c: (b, c, g, 0, 0)),
                  row_spec(v_w),
                  pl.BlockSpec((1, HEAD_DIM), lambda b, g, c: (0, 0))],
        out_specs=[row_spec(v_w),
                   pl.BlockSpec((None, heads_per_step, HEAD_DIM, HEAD_DIM),
                                lambda b, g, c: (b, g, 0, 0))],
        out_shape=[jax.ShapeDtypeStruct((m, value_dim), bf16),
                   jax.ShapeDtypeStruct((batch, n_heads, HEAD_DIM, HEAD_DIM), f32)],
        scratch_shapes=[pltpu.VMEM((heads_per_step, HEAD_DIM, HEAD_DIM), f32)],
        compiler_params=_params(("parallel", "parallel", "arbitrary")),
        name="gdn_chunked",
    )(qkv, qkv, qkv, bg, gc_rows, z, norm_g.reshape(1, HEAD_DIM))


def _gdn_step_kernel(*refs, n_heads):
    qt_ref, kt_ref, v_ref, beta_ref, g_ref, z_ref, ng_ref, s_ref = refs[:8]
    o_ref, sout_ref, kcol_ref, qcol_ref, eg_ref, oraw_ref = refs[-6:]
    n_k = qt_ref.shape[1]
    rep = n_heads // n_k
    for kh in range(n_k):
        kcol_ref[kh] = jnp.broadcast_to(kt_ref[:, kh:kh + 1], (HEAD_DIM, HEAD_DIM))
        qcol_ref[kh] = jnp.broadcast_to(qt_ref[:, kh:kh + 1], (HEAD_DIM, HEAD_DIM))
    eg_ref[...] = jnp.exp(g_ref[...])
    for h in range(n_heads):
        kcol = kcol_ref[h // rep]
        s = s_ref[h] * eg_ref[h:h + 1, :]
        kv = jnp.sum(kcol * s, axis=0, keepdims=True)
        delta = (v_ref[h:h + 1, :] - kv) * beta_ref[h:h + 1, :]
        s = s + kcol * delta
        sout_ref[h] = s
        oraw_ref[h:h + 1, :] = jnp.sum(qcol_ref[h // rep] * s, axis=0, keepdims=True)
    o = oraw_ref[...]
    on = o * lax.rsqrt(jnp.mean(o * o, axis=-1, keepdims=True) + EPS) * ng_ref[...]
    o_ref[...] = on * _silu(z_ref[...])


def _gdn_step(qt, kt, v3, beta3, g3, z3, norm_g, states, layer, new_states):
    b, n_heads, _ = v3.shape
    n_k = qt.shape[2]

    def head_spec():
        return pl.BlockSpec((None, n_heads, HEAD_DIM), lambda i: (i, 0, 0))

    def t_spec():
        return pl.BlockSpec((None, HEAD_DIM, n_k), lambda i: (i, 0, 0))

    s_spec = pl.BlockSpec((None, None, n_heads, HEAD_DIM, HEAD_DIM),
                          lambda i: (layer, i, 0, 0, 0))
    ns_shape, extra, extra_specs, alias = _stacked_out(new_states, layer, states.shape, f32)
    return pl.pallas_call(
        functools.partial(_gdn_step_kernel, n_heads=n_heads),
        grid=(b,),
        in_specs=[t_spec(), t_spec(), head_spec(), head_spec(), head_spec(), head_spec(),
                  pl.BlockSpec((1, HEAD_DIM), lambda i: (0, 0)), s_spec] + extra_specs,
        out_specs=[head_spec(), s_spec],
        out_shape=[jax.ShapeDtypeStruct((b, n_heads, HEAD_DIM), f32), ns_shape],
        scratch_shapes=[pltpu.VMEM((n_k, HEAD_DIM, HEAD_DIM), f32),
                        pltpu.VMEM((n_k, HEAD_DIM, HEAD_DIM), f32),
                        pltpu.VMEM((n_heads, HEAD_DIM), f32),
                        pltpu.VMEM((n_heads, HEAD_DIM), f32)],
        input_output_aliases=alias(8, 1),
        compiler_params=_params(("parallel",)),
        name="gdn_step",
    )(qt, kt, v3, beta3, g3, z3, norm_g.reshape(1, HEAD_DIM), states, *extra)


def _pool_layer(hp, hs, states, layer, new_states, w_in, w_grp, scale, w_out, norm_g, batch):
    e = w_grp.shape[1] * w_grp.shape[2]

    def project(h):
        xn = _rmsnorm(h, norm_g, bf16)
        return _matmul(xn, w_in, layer, 0, e), _matmul(xn, w_in, layer, e, e)

    def mix(h, d, z):
        return _out_project(h, _grouped_matmul_gate(d, w_grp, layer, scale, z), w_out, layer)

    u, z = project(hp)
    u3 = u.reshape(batch, -1, e)
    hp = mix(hp, _pool_diff_prompt(u3).reshape(u.shape), z)
    pool_p = u3[:, -POOL_CTX:]

    u, z = project(hs)
    d, new_states = _pool_diff_sample(states, layer, u, new_states)
    hs = mix(hs, d, z)
    return hp, hs, pool_p, new_states


def _gdn_layer(hp, hs, conv_states, ssm_states, layer, new_conv, new_ssm, w_in, conv_w, a_log,
               dt_bias, head_g, w_out, norm_g, batch):
    value_dim = w_out.shape[1]
    n_heads = value_dim // HEAD_DIM
    conv_dim = conv_w.shape[1]
    key_dim = (conv_dim - value_dim) // 2
    n_k = key_dim // HEAD_DIM

    def project(h):
        xn = _rmsnorm(h, norm_g, bf16)
        qkv = _matmul(xn, w_in, layer, 0, conv_dim)
        z = _matmul(xn, w_in, layer, conv_dim, value_dim)
        ba = _matmul(xn, w_in, layer, conv_dim + value_dim, 2 * n_heads)
        return qkv, z, ba

    qkv, z, ba = project(hp)
    t = hp.shape[0] // batch
    qkv3 = qkv.reshape(batch, t, conv_dim)
    act = _conv_prompt(qkv3, conv_w, key_dim).reshape(-1, conv_dim)
    a_pad, dt_pad = _gate_params(a_log, dt_bias)
    bg = _gates_chunked(ba.reshape(batch, t, 2 * n_heads), a_pad, dt_pad)
    gc = bg[..., n_heads:]
    gc_rows = gc.reshape(batch, t // CHUNK, CHUNK, n_heads // QUAD, QUAD)
    gc_rows = jnp.transpose(gc_rows, (0, 1, 3, 4, 2)).reshape(
        batch, t // CHUNK, n_heads // QUAD, 1, QUAD * CHUNK)
    n_step = QUAD * QUADS_PER_STEP
    bg = jnp.concatenate([bg[..., :n_heads].reshape(-1, n_heads // n_step, n_step),
                          gc.reshape(-1, n_heads // n_step, n_step)], axis=2)
    o, ssm_p = _gdn_chunked(act, jnp.transpose(bg, (1, 0, 2)), gc_rows, z, head_g, batch)
    hp = _out_project(hp, o, w_out, layer)
    conv_p = qkv3[:, -(CONV_WIDTH - 1):]

    qkv, z, ba = project(hs)
    bs = hs.shape[0]
    act, new_conv = _conv_sample(conv_states, layer, qkv, conv_w, key_dim, new_conv)
    beta_b, g_b = _gates_step(ba, a_pad, dt_pad)
    qt = jnp.transpose(act[:, :key_dim].reshape(bs, n_k, HEAD_DIM), (0, 2, 1))
    kt = jnp.transpose(act[:, key_dim:2 * key_dim].reshape(bs, n_k, HEAD_DIM), (0, 2, 1))
    per_head = (bs, n_heads, HEAD_DIM)
    o, new_ssm = _gdn_step(qt, kt, act[:, 2 * key_dim:].reshape(per_head),
                           beta_b.reshape(per_head), g_b.reshape(per_head), z.reshape(per_head),
                           head_g, ssm_states, layer, new_ssm)
    hs = _out_project(hs, o.reshape(bs, value_dim).astype(bf16), w_out, layer)
    return hp, hs, conv_p, ssm_p, new_conv, new_ssm


def kernel(x_prompt, x_sample, state_pool, state_conv, state_ssm, meta_tokens, norm_g, final_norm_g, pool_w_in, pool_w_grp, pool_scale, pool_w_out, gdn_w_in, gdn_conv_w, gdn_A_log, gdn_dt_bias, gdn_norm_g, gdn_w_out):
    dt = x_prompt.dtype
    batch, seq, d = x_prompt.shape
    depth = norm_g.shape[0]
    meta = jnp.broadcast_to(meta_tokens.astype(dt)[None], (batch, N_META, d))
    hp = jnp.concatenate([jnp.zeros((batch, FRONT, d), dt), meta, x_prompt], axis=1)
    t = hp.shape[1]
    hp = hp.reshape(batch * t, d)
    hs = x_sample.reshape(x_sample.shape[0], d)
    pool_p, conv_p, ssm_p = [], [], []
    pool_s = conv_s = ssm_s = None
    for i in range(depth):
        j = i // 2
        if i % 2 == 0:
            hp, hs, pp, pool_s = _pool_layer(hp, hs, state_pool, j, pool_s, pool_w_in, pool_w_grp,
                                             pool_scale[j], pool_w_out, norm_g[i], batch)
            pool_p.append(pp)
        else:
            hp, hs, cp, sp, conv_s, ssm_s = _gdn_layer(
                hp, hs, state_conv, state_ssm, j, conv_s, ssm_s, gdn_w_in, gdn_conv_w[j],
                gdn_A_log[j], gdn_dt_bias[j], gdn_norm_g[j], gdn_w_out, norm_g[i], batch)
            conv_p.append(cp)
            ssm_p.append(sp)
    y_prompt = _final_norm_prompt(hp.reshape(batch, t, d), final_norm_g, seq)
    y_sample = _rmsnorm(hs, final_norm_g, dt).reshape(x_sample.shape)
    return (y_prompt, y_sample, jnp.stack(pool_p), jnp.stack(conv_p), jnp.stack(ssm_p),
            pool_s, conv_s, ssm_s)
```

```python
import functools

import jax
import jax.numpy as jnp
from jax import lax
from jax.experimental import pallas as pl
from jax.experimental.pallas import tpu as pltpu

N_META = 16
PAST_LEN = 16384
POOL_WINDOWS = (2, 4, 8, 16)
POOL_CTX = max(POOL_WINDOWS) - 1
HEAD_DIM = 128
CONV_WIDTH = 4
CHUNK = 64
EPS = 1e-6
FRONT = (-N_META) % CHUNK
QUAD = 4
QUADS_PER_STEP = 8
NEUMANN_LEVELS = 6

VMEM_LIMIT = 56 * 1024 * 1024
SUBLANES = 8

f32 = jnp.float32
bf16 = jnp.bfloat16


def _params(sem):
    return pltpu.CompilerParams(dimension_semantics=sem, vmem_limit_bytes=VMEM_LIMIT)


def _silu(x):
    return x * jax.nn.sigmoid(x)


def _stacked_out(stack, layer, shape, dtype):
    if stack is None:
        return jax.ShapeDtypeStruct(shape, dtype), [], [], lambda n_in, n_out: {}
    return (jax.ShapeDtypeStruct(stack.shape, stack.dtype), [stack],
            [pl.BlockSpec(memory_space=pl.ANY)], lambda n_in, n_out: {n_in: n_out})


def _rmsnorm_kernel(x_ref, g_ref, o_ref):
    x = x_ref[...]
    y = x * lax.rsqrt(jnp.mean(x * x, axis=-1, keepdims=True) + EPS)
    o_ref[...] = (y * g_ref[...]).astype(o_ref.dtype)


def _rmsnorm(x, g, out_dtype):
    m, d = x.shape
    tm = 256 if m % 256 == 0 else m
    return pl.pallas_call(
        _rmsnorm_kernel,
        grid=(m // tm,),
        in_specs=[pl.BlockSpec((tm, d), lambda i: (i, 0)),
                  pl.BlockSpec((1, d), lambda i: (0, 0))],
        out_specs=pl.BlockSpec((tm, d), lambda i: (i, 0)),
        out_shape=jax.ShapeDtypeStruct((m, d), out_dtype),
        compiler_params=_params(("parallel",)),
        name="rmsnorm",
    )(x, g.reshape(1, d))


def _final_norm_prompt(h3, g, seq):
    b, t, d = h3.shape
    skip = (t - seq) // CHUNK
    return pl.pallas_call(
        _rmsnorm_kernel,
        grid=(b, seq // CHUNK),
        in_specs=[pl.BlockSpec((None, CHUNK, d), lambda i, j: (i, j + skip, 0)),
                  pl.BlockSpec((1, d), lambda i, j: (0, 0))],
        out_specs=pl.BlockSpec((None, CHUNK, d), lambda i, j: (i, j, 0)),
        out_shape=jax.ShapeDtypeStruct((b, seq, d), h3.dtype),
        compiler_params=_params(("parallel", "parallel")),
        name="final_norm",
    )(h3, g.reshape(1, d))


MM_K = 4096
MM_TN = 512


def _mm_kernel(*refs, epilogue, m_axis):
    x_ref, w_ref = refs[0], refs[1]
    o_ref, w16_ref = refs[-2], refs[-1]

    @pl.when(pl.program_id(m_axis) == 0)
    def _():
        w16_ref[...] = w_ref[...].astype(bf16)

    acc = jnp.dot(x_ref[...], w16_ref[...], preferred_element_type=f32)
    if epilogue == "plain":
        o_ref[...] = acc.astype(o_ref.dtype)
    elif epilogue == "gate":
        scale_ref, z_ref = refs[2], refs[3]
        o_ref[...] = (acc * scale_ref[...] * _silu(z_ref[...])).astype(o_ref.dtype)
    elif epilogue == "residual":
        h_ref = refs[2]
        o_ref[...] = h_ref[...] + acc
    else:
        raise ValueError(epilogue)


def _mm_rows(m, epilogue):
    for tm in ((1056,) if epilogue == "residual" else (1408, 1056)):
        if m % tm == 0:
            return tm
    return m


def _matmul(x, w, layer, col0, ncols, kblock=0, out_dtype=f32, residual=None):
    m = x.shape[0]
    tk = min(MM_K, x.shape[1])
    epilogue = "plain" if residual is None else "residual"
    tm = _mm_rows(m, epilogue)
    tn = min(MM_TN, ncols)
    off = col0 // tn
    assert col0 % tn == 0 and ncols % tn == 0 and x.shape[1] % tk == 0
    in_specs = [pl.BlockSpec((tm, tk), lambda j, i: (i, kblock)),
                pl.BlockSpec((None, tk, tn), lambda j, i: (layer, kblock, j + off))]
    args = [x, w]
    if residual is not None:
        in_specs.append(pl.BlockSpec((tm, tn), lambda j, i: (i, j)))
        args.append(residual)
    return pl.pallas_call(
        functools.partial(_mm_kernel, epilogue=epilogue, m_axis=1),
        grid=(ncols // tn, m // tm),
        in_specs=in_specs,
        out_specs=pl.BlockSpec((tm, tn), lambda j, i: (i, j)),
        out_shape=jax.ShapeDtypeStruct((m, ncols), out_dtype),
        scratch_shapes=[pltpu.VMEM((tk, tn), bf16)],
        compiler_params=_params(("parallel", "arbitrary")),
        name="matmul",
    )(*args)


def _out_project(h, act, w_out, layer):
    tk = min(MM_K, act.shape[1])
    for kb in range(act.shape[1] // tk):
        h = _matmul(act, w_out, layer, 0, w_out.shape[2], kblock=kb, residual=h)
    return h


def _grouped_matmul_gate(d, w_grp, layer, scale, z):
    m, e = d.shape
    _, ng, gk, gn = w_grp.shape
    tm = _mm_rows(m, "gate")
    tn = min(2 * MM_TN, gn)
    nb = gn // tn
    return pl.pallas_call(
        functools.partial(_mm_kernel, epilogue="gate", m_axis=2),
        grid=(ng, nb, m // tm),
        in_specs=[pl.BlockSpec((tm, gk), lambda g, j, i: (i, g)),
                  pl.BlockSpec((None, None, gk, tn), lambda g, j, i: (layer, g, 0, j)),
                  pl.BlockSpec((1, tn), lambda g, j, i: (0, g * nb + j)),
                  pl.BlockSpec((tm, tn), lambda g, j, i: (i, g * nb + j))],
        out_specs=pl.BlockSpec((tm, tn), lambda g, j, i: (i, g * nb + j)),
        out_shape=jax.ShapeDtypeStruct((m, e), bf16),
        scratch_shapes=[pltpu.VMEM((gk, tn), bf16)],
        compiler_params=_params(("parallel", "parallel", "arbitrary")),
        name="grouped_matmul_gate",
    )(d, w_grp, scale.reshape(1, e), z)


POOL_HALO = 16
POOL_ROWS = 16
POOL_LANES = 512


def _row_pieces(x):
    return [x[r:r + SUBLANES] for r in range(0, x.shape[0], SUBLANES)]


def _shift_rows(pieces, d):
    if d == SUBLANES:
        return [pieces[0]] + pieces[:-1]
    rolled = [pltpu.roll(p, d, 0) for p in pieces]
    keep = lax.broadcasted_iota(jnp.int32, pieces[0].shape, 0) >= d
    return [rolled[0]] + [jnp.where(keep, rolled[k], rolled[k - 1]) for k in range(1, len(pieces))]


def _window_sums(pieces, w):
    d = 1
    while d < w:
        pieces = [a + b for a, b in zip(pieces, _shift_rows(pieces, d))]
        d *= 2
    return pieces


def _pool_diff_prompt_kernel(halo_ref, cur_ref, o_ref, *, tm, gsz):
    t = pl.program_id(1)
    gi = (pl.program_id(2) * POOL_LANES) // gsz
    for widx, w in enumerate(POOL_WINDOWS):
        ctx = -(-(w - 1) // SUBLANES) * SUBLANES

        @pl.when(gi == widx)
        def _(w=w, ctx=ctx):
            for r in range(0, tm, POOL_ROWS):
                cur = cur_ref[r:r + POOL_ROWS, :]
                if r == 0:
                    prev = jnp.where(t > 0, halo_ref[POOL_HALO - ctx:POOL_HALO, :], 0.0)
                else:
                    prev = cur_ref[r - ctx:r, :]
                s = _window_sums(_row_pieces(prev) + _row_pieces(cur), w)[ctx // SUBLANES:]
                s = jnp.concatenate(s, axis=0)
                pos = lax.broadcasted_iota(jnp.int32, (POOL_ROWS, 1), 0) + (t * tm + r - FRONT)
                cnt = jnp.clip(pos + 1, 1, w).astype(f32)
                o_ref[r:r + POOL_ROWS, :] = (s / cnt - cur).astype(o_ref.dtype)


def _pool_diff_prompt(u3):
    b, t, e = u3.shape
    tm = 1056
    assert t % tm == 0 and tm % POOL_HALO == 0 and e % (len(POOL_WINDOWS) * POOL_LANES) == 0
    hb = tm // POOL_HALO
    return pl.pallas_call(
        functools.partial(_pool_diff_prompt_kernel, tm=tm, gsz=e // len(POOL_WINDOWS)),
        grid=(b, t // tm, e // POOL_LANES),
        in_specs=[pl.BlockSpec((None, POOL_HALO, POOL_LANES),
                               lambda i, j, c: (i, jnp.maximum(j * hb - 1, 0), c)),
                  pl.BlockSpec((None, tm, POOL_LANES), lambda i, j, c: (i, j, c))],
        out_specs=pl.BlockSpec((None, tm, POOL_LANES), lambda i, j, c: (i, j, c)),
        out_shape=jax.ShapeDtypeStruct((b, t, e), bf16),
        compiler_params=_params(("parallel", "parallel", "parallel")),
        name="pool_diff_prompt",
    )(u3, u3)


def _pool_diff_sample_kernel(*refs, gsz, lanes):
    st_ref, u_ref = refs[0], refs[1]
    o_ref, ns_ref = refs[-2], refs[-1]
    gi = (pl.program_id(1) * lanes) // gsz
    for widx, w in enumerate(POOL_WINDOWS):
        @pl.when(gi == widx)
        def _(w=w):
            cur = u_ref[...]
            s = cur
            for i in range(1, w):
                s = s + st_ref[:, POOL_CTX - i, :]
            cnt = float(min(PAST_LEN + 1, w))
            o_ref[...] = (s / cnt - cur).astype(o_ref.dtype)
    for r in range(POOL_CTX - 1):
        ns_ref[:, r, :] = st_ref[:, r + 1, :]
    ns_ref[:, POOL_CTX - 1, :] = u_ref[...]


def _pool_diff_sample(states, layer, u, new_states):
    _, b, _, e = states.shape
    gsz = e // len(POOL_WINDOWS)
    bb, lanes = 16, min(1024, gsz)
    assert b % bb == 0 and gsz % lanes == 0
    st_spec = pl.BlockSpec((None, bb, POOL_CTX, lanes), lambda i, c: (layer, i, 0, c))
    ns_shape, extra, extra_specs, alias = _stacked_out(new_states, layer, states.shape, states.dtype)
    return pl.pallas_call(
        functools.partial(_pool_diff_sample_kernel, gsz=gsz, lanes=lanes),
        grid=(b // bb, e // lanes),
        in_specs=[st_spec, pl.BlockSpec((bb, lanes), lambda i, c: (i, c))] + extra_specs,
        out_specs=[pl.BlockSpec((bb, lanes), lambda i, c: (i, c)), st_spec],
        out_shape=[jax.ShapeDtypeStruct((b, e), bf16), ns_shape],
        input_output_aliases=alias(2, 1),
        compiler_params=_params(("parallel", "parallel")),
        name="pool_diff_sample",
    )(states, u, *extra)


CONV_HALO = 8
CONV_ROWS = 16
CONV_LANES = 1024


def _conv_act(acc, scale):
    y = _silu(acc)
    if scale is None:
        return y
    parts = []
    for h in range(y.shape[1] // HEAD_DIM):
        yh = y[:, h * HEAD_DIM:(h + 1) * HEAD_DIM]
        parts.append(yh * lax.rsqrt(jnp.sum(yh * yh, axis=-1, keepdims=True) + EPS) * scale)
    return jnp.concatenate(parts, axis=1)


def _conv_branches(c, key_dim, body):
    normalise = c * CONV_LANES < 2 * key_dim
    q_scale = jnp.where(c * CONV_LANES < key_dim, HEAD_DIM ** -0.5, 1.0)
    pl.when(normalise)(lambda: body(q_scale))
    pl.when(jnp.logical_not(normalise))(lambda: body(None))


def _conv_prompt_kernel(halo_ref, cur_ref, w_ref, o_ref, *, tm, key_dim):
    t = pl.program_id(1)

    def body(scale):
        for r in range(0, tm, CONV_ROWS):
            if r == 0:
                prev = jnp.where(t > 0, halo_ref[...], 0.0)
            else:
                prev = cur_ref[r - CONV_HALO:r, :]
            cur = cur_ref[r:r + CONV_ROWS, :]
            pieces = _row_pieces(prev) + _row_pieces(cur)
            acc = cur * w_ref[CONV_WIDTH - 1:CONV_WIDTH, :]
            for d in range(1, CONV_WIDTH):
                tap = CONV_WIDTH - 1 - d
                back = jnp.concatenate(_shift_rows(pieces, d)[CONV_HALO // SUBLANES:], axis=0)
                acc = acc + back * w_ref[tap:tap + 1, :]
            o_ref[r:r + CONV_ROWS, :] = _conv_act(acc, scale)

    _conv_branches(pl.program_id(2), key_dim, body)


def _conv_prompt(x3, conv_w, key_dim):
    b, t, c = x3.shape
    tm = 1056
    assert t % tm == 0 and tm % CONV_ROWS == 0 and tm % CONV_HALO == 0 and key_dim % CONV_LANES == 0
    hb = tm // CONV_HALO
    return pl.pallas_call(
        functools.partial(_conv_prompt_kernel, tm=tm, key_dim=key_dim),
        grid=(b, t // tm, c // CONV_LANES),
        in_specs=[pl.BlockSpec((None, CONV_HALO, CONV_LANES),
                               lambda i, j, l: (i, jnp.maximum(j * hb - 1, 0), l)),
                  pl.BlockSpec((None, tm, CONV_LANES), lambda i, j, l: (i, j, l)),
                  pl.BlockSpec((CONV_WIDTH, CONV_LANES), lambda i, j, l: (0, l))],
        out_specs=pl.BlockSpec((None, tm, CONV_LANES), lambda i, j, l: (i, j, l)),
        out_shape=jax.ShapeDtypeStruct((b, t, c), f32),
        compiler_params=_params(("parallel", "parallel", "parallel")),
        name="conv_prompt",
    )(x3, x3, conv_w)


def _conv_sample_kernel(*refs, key_dim):
    st_ref, x_ref, w_ref = refs[0], refs[1], refs[2]
    o_ref, ns_ref = refs[-2], refs[-1]

    def body(scale):
        acc = x_ref[...] * w_ref[CONV_WIDTH - 1:CONV_WIDTH, :]
        for i in range(CONV_WIDTH - 1):
            acc = acc + st_ref[:, i, :] * w_ref[i:i + 1, :]
        o_ref[...] = _conv_act(acc, scale)

    _conv_branches(pl.program_id(0), key_dim, body)
    for r in range(CONV_WIDTH - 2):
        ns_ref[:, r, :] = st_ref[:, r + 1, :]
    ns_ref[:, CONV_WIDTH - 2, :] = x_ref[...]


def _conv_sample(states, layer, x, conv_w, key_dim, new_states):
    b, c = x.shape
    st_spec = pl.BlockSpec((None, b, CONV_WIDTH - 1, CONV_LANES), lambda l: (layer, 0, 0, l))
    ns_shape, extra, extra_specs, alias = _stacked_out(new_states, layer, states.shape, states.dtype)
    return pl.pallas_call(
        functools.partial(_conv_sample_kernel, key_dim=key_dim),
        grid=(c // CONV_LANES,),
        in_specs=[st_spec,
                  pl.BlockSpec((b, CONV_LANES), lambda l: (0, l)),
                  pl.BlockSpec((CONV_WIDTH, CONV_LANES), lambda l: (0, l))] + extra_specs,
        out_specs=[pl.BlockSpec((b, CONV_LANES), lambda l: (0, l)), st_spec],
        out_shape=[jax.ShapeDtypeStruct((b, c), f32), ns_shape],
        input_output_aliases=alias(3, 1),
        compiler_params=_params(("parallel",)),
        name="conv_sample",
    )(states, x, conv_w, *extra)


def _gate_values(ba_ref, a_ref, dt_ref):
    x = ba_ref[...]
    xa = x + dt_ref[...]
    softplus = jnp.maximum(xa, 0.0) + jnp.log1p(jnp.exp(-jnp.abs(xa)))
    return jax.nn.sigmoid(x), -jnp.exp(a_ref[...]) * softplus


def _gate_chunk_kernel(ba_ref, a_ref, dt_ref, o_ref, *, n_heads):
    beta, g = _gate_values(ba_ref, a_ref, dt_ref)
    row = lax.broadcasted_iota(jnp.int32, (CHUNK, 1), 0)
    g = jnp.where(row < jnp.where(pl.program_id(1) == 0, FRONT, 0), 0.0, g)
    shift = 1
    while shift < CHUNK:
        g = g + jnp.where(row >= shift, pltpu.roll(g, shift, 0), 0.0)
        shift *= 2
    lane = lax.broadcasted_iota(jnp.int32, g.shape, 1)
    o_ref[...] = jnp.where(lane < n_heads, beta, g)


def _gates_chunked(ba3, a_pad, dt_pad):
    b, t, w = ba3.shape
    return pl.pallas_call(
        functools.partial(_gate_chunk_kernel, n_heads=w // 2),
        grid=(b, t // CHUNK),
        in_specs=[pl.BlockSpec((None, CHUNK, w), lambda i, j: (i, j, 0)),
                  pl.BlockSpec((1, w), lambda i, j: (0, 0)),
                  pl.BlockSpec((1, w), lambda i, j: (0, 0))],
        out_specs=pl.BlockSpec((None, CHUNK, w), lambda i, j: (i, j, 0)),
        out_shape=jax.ShapeDtypeStruct((b, t, w), f32),
        compiler_params=_params(("parallel", "parallel")),
        name="gdn_gates_chunked",
    )(ba3, a_pad, dt_pad)


def _gate_step_kernel(ba_ref, a_ref, dt_ref, beta_ref, g_ref, *, n_heads):
    beta, g = _gate_values(ba_ref, a_ref, dt_ref)
    rows = beta.shape[0]
    for h in range(n_heads):
        sl = slice(h * HEAD_DIM, (h + 1) * HEAD_DIM)
        beta_ref[:, sl] = jnp.broadcast_to(beta[:, h:h + 1], (rows, HEAD_DIM))
        g_ref[:, sl] = jnp.broadcast_to(g[:, n_heads + h:n_heads + h + 1], (rows, HEAD_DIM))


def _gates_step(ba, a_pad, dt_pad):
    b, w = ba.shape
    n_heads = w // 2
    wide = n_heads * HEAD_DIM
    return pl.pallas_call(
        functools.partial(_gate_step_kernel, n_heads=n_heads),
        out_shape=[jax.ShapeDtypeStruct((b, wide), f32), jax.ShapeDtypeStruct((b, wide), f32)],
        compiler_params=pltpu.CompilerParams(vmem_limit_bytes=VMEM_LIMIT),
        name="gdn_gates_step",
    )(ba, a_pad, dt_pad)


def _gate_params(a_log, dt_bias):
    zeros = jnp.zeros(a_log.shape, f32)
    w = 2 * a_log.shape[0]
    return (jnp.concatenate([zeros, a_log.astype(f32)]).reshape(1, w),
            jnp.concatenate([zeros, dt_bias.astype(f32)]).reshape(1, w))


def _dot_nt(a, b):
    return lax.dot_general(a, b, (((1,), (1,)), ((), ())), preferred_element_type=f32)


def _dot_tn(a, b):
    return lax.dot_general(a, b, (((0,), (0,)), ((), ())), preferred_element_type=f32)


def _dot16(a, b):
    return jnp.dot(a.astype(bf16), b.astype(bf16), preferred_element_type=f32)


def _stack(x, idx):
    return jnp.concatenate([x[:, i * HEAD_DIM:(i + 1) * HEAD_DIM] for i in idx], axis=0)


def _gdn_chunk_kernel(q_ref, k_ref, v_ref, bg_ref, gcr_ref, z_ref, ng_ref,
                      o_ref, sout_ref, s_ref):
    c = pl.program_id(2)

    @pl.when(c == 0)
    def _():
        s_ref[...] = jnp.zeros_like(s_ref)

    quads = range(QUADS_PER_STEP)
    heads = tuple(range(QUAD))
    khead = tuple(h // 2 for h in heads)
    rows = QUAD * CHUNK
    kq_w = QUAD // 2 * HEAD_DIM
    v_w = QUAD * HEAD_DIM

    def head_rows(h):
        return slice(h * CHUNK, (h + 1) * CHUNK)

    ri = lax.broadcasted_iota(jnp.int32, (rows, rows), 0)
    ci = lax.broadcasted_iota(jnp.int32, (rows, rows), 1)
    shift = CHUNK.bit_length() - 1
    same = lax.shift_right_logical(ri, shift) == lax.shift_right_logical(ci, shift)
    lower = same & (ri >= ci)
    strict = ri > ci

    bg = bg_ref[...]
    n_step = QUAD * QUADS_PER_STEP
    k16, kb16, q16, rhs, qd, kd16, decay, g_last = [], [], [], [], [], [], [], []
    for i in quads:
        kq = slice(i * kq_w, (i + 1) * kq_w)
        vs = slice(i * v_w, (i + 1) * v_w)
        kst = _stack(k_ref[:, kq], khead)
        qst = _stack(q_ref[:, kq], khead)
        cols = [i * QUAD + h for h in heads]
        bcol = [jnp.broadcast_to(bg[:, n:n + 1], (CHUNK, HEAD_DIM)) for n in cols]
        gcol = [jnp.broadcast_to(bg[:, n_step + n:n_step + n + 1], (CHUNK, HEAD_DIM)) for n in cols]
        bst = jnp.concatenate(bcol, axis=0)
        gst = jnp.concatenate(gcol, axis=0)
        last = [g[CHUNK - 1:CHUNK] for g in gcol]
        glast = jnp.concatenate([jnp.broadcast_to(r, (CHUNK, HEAD_DIM)) for r in last], axis=0)
        eg = jnp.exp(gst)
        kb = kst * bst
        k16.append(kst.astype(bf16))
        kb16.append(kb.astype(bf16))
        q16.append(qst.astype(bf16))
        rhs.append(jnp.concatenate([_stack(v_ref[:, vs], heads) * bst, kb * eg], axis=1))
        qd.append(qst * eg)
        kd16.append((kst * jnp.exp(glast - gst)).astype(bf16))
        diff = jnp.concatenate([gst] * (rows // HEAD_DIM), axis=1) - gcr_ref[i]
        decay.append(jnp.exp(jnp.where(lower, diff, -jnp.inf)))
        g_last.append([jnp.exp(r) for r in last])

    pw = [-jnp.where(strict, _dot_nt(kb16[i], k16[i]) * decay[i], 0.0) for i in quads]
    attn16 = [(_dot_nt(q16[i], k16[i]) * decay[i]).astype(bf16) for i in quads]

    sol = [rhs[i] + _dot16(pw[i], rhs[i]) for i in quads]
    for _ in range(NEUMANN_LEVELS - 1):
        pw16 = [p.astype(bf16) for p in pw]
        pw = [jnp.dot(p, p, preferred_element_type=f32) for p in pw16]
        sol = [sol[i] + _dot16(pw[i], sol[i]) for i in quads]

    x = [[_dot16(jnp.concatenate([sol[i][head_rows(h), HEAD_DIM:], qd[i][head_rows(h)]], axis=0),
                 s_ref[i * QUAD + h]) for h in heads] for i in quads]
    vn16 = [jnp.concatenate([sol[i][head_rows(h), :HEAD_DIM] - x[i][h][:CHUNK] for h in heads],
                            axis=0).astype(bf16) for i in quads]
    o = [jnp.concatenate([x[i][h][CHUNK:] for h in heads], axis=0)
         + jnp.dot(attn16[i], vn16[i], preferred_element_type=f32) for i in quads]
    for i in quads:
        for h in heads:
            s_ref[i * QUAD + h] = (s_ref[i * QUAD + h] * g_last[i][h]
                                   + _dot_tn(kd16[i][head_rows(h)], vn16[i][head_rows(h)]))

    for i in quads:
        on = o[i] * lax.rsqrt(jnp.mean(o[i] * o[i], axis=-1, keepdims=True) + EPS) * ng_ref[...]
        for h in heads:
            sl = slice(i * v_w + h * HEAD_DIM, i * v_w + (h + 1) * HEAD_DIM)
            o_ref[:, sl] = (on[head_rows(h)] * _silu(z_ref[:, sl])).astype(o_ref.dtype)

    @pl.when(c == pl.num_programs(2) - 1)
    def _():
        sout_ref[...] = s_ref[...]


def _gdn_chunked(qkv, bg, gc_rows, z, norm_g, batch):
    m, value_dim = z.shape
    key_dim = (qkv.shape[1] - value_dim) // 2
    n_heads = value_dim // HEAD_DIM
    n_chunks = m // batch // CHUNK
    heads_per_step = QUAD * QUADS_PER_STEP
    n_groups = n_heads // heads_per_step
    kq_w = heads_per_step // 2 * HEAD_DIM
    v_w = heads_per_step * HEAD_DIM

    def row_spec(width, col0=0):
        off = col0 // width
        return pl.BlockSpec((CHUNK, width), lambda b, g, c: (b * n_chunks + c, g + off))

    return pl.pallas_call(
        _gdn_chunk_kernel,
        grid=(batch, n_groups, n_chunks),
        in_specs=[row_spec(kq_w), row_spec(kq_w, key_dim), row_spec(v_w, 2 * key_dim),
                  pl.BlockSpec((None, CHUNK, 2 * heads_per_step),
                               lambda b, g, c: (g, b * n_chunks + c, 0)),
                  pl.BlockSpec((None, None, QUADS_PER_STEP, 1, QUAD * CHUNK),
                               lambda b, g, c: (b, c, g, 0, 0)),
                  row_spec(v_w),
                  pl.BlockSpec((1, HEAD_DIM), lambda b, g, c: (0, 0))],
        out_specs=[row_spec(v_w),
                   pl.BlockSpec((None, heads_per_step, HEAD_DIM, HEAD_DIM),
                                lambda b, g, c: (b, g, 0, 0))],
        out_shape=[jax.ShapeDtypeStruct((m, value_dim), bf16),
                   jax.ShapeDtypeStruct((batch, n_heads, HEAD_DIM, HEAD_DIM), f32)],
        scratch_shapes=[pltpu.VMEM((heads_per_step, HEAD_DIM, HEAD_DIM), f32)],
        compiler_params=_params(("parallel", "parallel", "arbitrary")),
        name="gdn_chunked",
    )(qkv, qkv, qkv, bg, gc_rows, z, norm_g.reshape(1, HEAD_DIM))


def _gdn_step_kernel(*refs, n_heads):
    qt_ref, kt_ref, v_ref, beta_ref, g_ref, z_ref, ng_ref, s_ref = refs[:8]
    o_ref, sout_ref, kcol_ref, qcol_ref, eg_ref, oraw_ref = refs[-6:]
    n_k = qt_ref.shape[1]
    rep = n_heads // n_k
    for kh in range(n_k):
        kcol_ref[kh] = jnp.broadcast_to(kt_ref[:, kh:kh + 1], (HEAD_DIM, HEAD_DIM))
        qcol_ref[kh] = jnp.broadcast_to(qt_ref[:, kh:kh + 1], (HEAD_DIM, HEAD_DIM))
    eg_ref[...] = jnp.exp(g_ref[...])
    for h in range(n_heads):
        kcol = kcol_ref[h // rep]
        s = s_ref[h] * eg_ref[h:h + 1, :]
        kv = jnp.sum(kcol * s, axis=0, keepdims=True)
        delta = (v_ref[h:h + 1, :] - kv) * beta_ref[h:h + 1, :]
        s = s + kcol * delta
        sout_ref[h] = s
        oraw_ref[h:h + 1, :] = jnp.sum(qcol_ref[h // rep] * s, axis=0, keepdims=True)
    o = oraw_ref[...]
    on = o * lax.rsqrt(jnp.mean(o * o, axis=-1, keepdims=True) + EPS) * ng_ref[...]
    o_ref[...] = on * _silu(z_ref[...])


def _gdn_step(qt, kt, v3, beta3, g3, z3, norm_g, states, layer, new_states):
    b, n_heads, _ = v3.shape
    n_k = qt.shape[2]

    def head_spec():
        return pl.BlockSpec((None, n_heads, HEAD_DIM), lambda i: (i, 0, 0))

    def t_spec():
        return pl.BlockSpec((None, HEAD_DIM, n_k), lambda i: (i, 0, 0))

    s_spec = pl.BlockSpec((None, None, n_heads, HEAD_DIM, HEAD_DIM),
                          lambda i: (layer, i, 0, 0, 0))
    ns_shape, extra, extra_specs, alias = _stacked_out(new_states, layer, states.shape, f32)
    return pl.pallas_call(
        functools.partial(_gdn_step_kernel, n_heads=n_heads),
        grid=(b,),
        in_specs=[t_spec(), t_spec(), head_spec(), head_spec(), head_spec(), head_spec(),
                  pl.BlockSpec((1, HEAD_DIM), lambda i: (0, 0)), s_spec] + extra_specs,
        out_specs=[head_spec(), s_spec],
        out_shape=[jax.ShapeDtypeStruct((b, n_heads, HEAD_DIM), f32), ns_shape],
        scratch_shapes=[pltpu.VMEM((n_k, HEAD_DIM, HEAD_DIM), f32),
                        pltpu.VMEM((n_k, HEAD_DIM, HEAD_DIM), f32),
                        pltpu.VMEM((n_heads, HEAD_DIM), f32),
                        pltpu.VMEM((n_heads, HEAD_DIM), f32)],
        input_output_aliases=alias(8, 1),
        compiler_params=_params(("parallel",)),
        name="gdn_step",
    )(qt, kt, v3, beta3, g3, z3, norm_g.reshape(1, HEAD_DIM), states, *extra)


def _pool_layer(hp, hs, states, layer, new_states, w_in, w_grp, scale, w_out, norm_g, batch):
    e = w_grp.shape[1] * w_grp.shape[2]

    def project(h):
        xn = _rmsnorm(h, norm_g, bf16)
        return _matmul(xn, w_in, layer, 0, e), _matmul(xn, w_in, layer, e, e)

    def mix(h, d, z):
        return _out_project(h, _grouped_matmul_gate(d, w_grp, layer, scale, z), w_out, layer)

    u, z = project(hp)
    u3 = u.reshape(batch, -1, e)
    hp = mix(hp, _pool_diff_prompt(u3).reshape(u.shape), z)
    pool_p = u3[:, -POOL_CTX:]

    u, z = project(hs)
    d, new_states = _pool_diff_sample(states, layer, u, new_states)
    hs = mix(hs, d, z)
    return hp, hs, pool_p, new_states


def _gdn_layer(hp, hs, conv_states, ssm_states, layer, new_conv, new_ssm, w_in, conv_w, a_log,
               dt_bias, head_g, w_out, norm_g, batch):
    value_dim = w_out.shape[1]
    n_heads = value_dim // HEAD_DIM
    conv_dim = conv_w.shape[1]
    key_dim = (conv_dim - value_dim) // 2
    n_k = key_dim // HEAD_DIM

    def project(h):
        xn = _rmsnorm(h, norm_g, bf16)
        qkv = _matmul(xn, w_in, layer, 0, conv_dim)
        z = _matmul(xn, w_in, layer, conv_dim, value_dim)
        ba = _matmul(xn, w_in, layer, conv_dim + value_dim, 2 * n_heads)
        return qkv, z, ba

    qkv, z, ba = project(hp)
    t = hp.shape[0] // batch
    qkv3 = qkv.reshape(batch, t, conv_dim)
    act = _conv_prompt(qkv3, conv_w, key_dim).reshape(-1, conv_dim)
    a_pad, dt_pad = _gate_params(a_log, dt_bias)
    bg = _gates_chunked(ba.reshape(batch, t, 2 * n_heads), a_pad, dt_pad)
    gc = bg[..., n_heads:]
    gc_rows = gc.reshape(batch, t // CHUNK, CHUNK, n_heads // QUAD, QUAD)
    gc_rows = jnp.transpose(gc_rows, (0, 1, 3, 4, 2)).reshape(
        batch, t // CHUNK, n_heads // QUAD, 1, QUAD * CHUNK)
    n_step = QUAD * QUADS_PER_STEP
    bg = jnp.concatenate([bg[..., :n_heads].reshape(-1, n_heads // n_step, n_step),
                          gc.reshape(-1, n_heads // n_step, n_step)], axis=2)
    o, ssm_p = _gdn_chunked(act, jnp.transpose(bg, (1, 0, 2)), gc_rows, z, head_g, batch)
    hp = _out_project(hp, o, w_out, layer)
    conv_p = qkv3[:, -(CONV_WIDTH - 1):]

    qkv, z, ba = project(hs)
    bs = hs.shape[0]
    act, new_conv = _conv_sample(conv_states, layer, qkv, conv_w, key_dim, new_conv)
    beta_b, g_b = _gates_step(ba, a_pad, dt_pad)
    qt = jnp.transpose(act[:, :key_dim].reshape(bs, n_k, HEAD_DIM), (0, 2, 1))
    kt = jnp.transpose(act[:, key_dim:2 * key_dim].reshape(bs, n_k, HEAD_DIM), (0, 2, 1))
    per_head = (bs, n_heads, HEAD_DIM)
    o, new_ssm = _gdn_step(qt, kt, act[:, 2 * key_dim:].reshape(per_head),
                           beta_b.reshape(per_head), g_b.reshape(per_head), z.reshape(per_head),
                           head_g, ssm_states, layer, new_ssm)
    hs = _out_project(hs, o.reshape(bs, value_dim).astype(bf16), w_out, layer)
    return hp, hs, conv_p, ssm_p, new_conv, new_ssm


def kernel(x_prompt, x_sample, state_pool, state_conv, state_ssm, meta_tokens, norm_g, final_norm_g, pool_w_in, pool_w_grp, pool_scale, pool_w_out, gdn_w_in, gdn_conv_w, gdn_A_log, gdn_dt_bias, gdn_norm_g, gdn_w_out):
    dt = x_prompt.dtype
    batch, seq, d = x_prompt.shape
    depth = norm_g.shape[0]
    meta = jnp.broadcast_to(meta_tokens.astype(dt)[None], (batch, N_META, d))
    hp = jnp.concatenate([jnp.zeros((batch, FRONT, d), dt), meta, x_prompt], axis=1)
    t = hp.shape[1]
    hp = hp.reshape(batch * t, d)
    hs = x_sample.reshape(x_sample.shape[0], d)
    pool_p, conv_p, ssm_p = [], [], []
    pool_s = conv_s = ssm_s = None
    for i in range(depth):
        j = i // 2
        if i % 2 == 0:
            hp, hs, pp, pool_s = _pool_layer(hp, hs, state_pool, j, pool_s, pool_w_in, pool_w_grp,
                                             pool_scale[j], pool_w_out, norm_g[i], batch)
            pool_p.append(pp)
        else:
            hp, hs, cp, sp, conv_s, ssm_s = _gdn_layer(
                hp, hs, state_conv, state_ssm, j, conv_s, ssm_s, gdn_w_in, gdn_conv_w[j],
                gdn_A_log[j], gdn_dt_bias[j], gdn_norm_g[j], gdn_w_out, norm_g[i], batch)
            conv_p.append(cp)
            ssm_p.append(sp)
    y_prompt = _final_norm_prompt(hp.reshape(batch, t, d), final_norm_g, seq)
    y_sample = _rmsnorm(hs, final_norm_g, dt).reshape(x_sample.shape)
    return (y_prompt, y_sample, jnp.stack(pool_p), jnp.stack(conv_p), jnp.stack(ssm_p),
            pool_s, conv_s, ssm_s)
```

```python
import functools

import jax
import jax.numpy as jnp
from jax import lax
from jax.experimental import pallas as pl
from jax.experimental.pallas import tpu as pltpu

N_META = 16
PAST_LEN = 16384
POOL_WINDOWS = (2, 4, 8, 16)
POOL_CTX = max(POOL_WINDOWS) - 1
HEAD_DIM = 128
CONV_WIDTH = 4
CHUNK = 64
EPS = 1e-6
FRONT = (-N_META) % CHUNK
QUAD = 4
QUADS_PER_STEP = 8
NEUMANN_LEVELS = 6

VMEM_LIMIT = 56 * 1024 * 1024
SUBLANES = 8

f32 = jnp.float32
bf16 = jnp.bfloat16


def _params(sem):
    return pltpu.CompilerParams(dimension_semantics=sem, vmem_limit_bytes=VMEM_LIMIT)


def _silu(x):
    return x * jax.nn.sigmoid(x)


def _stacked_out(stack, layer, shape, dtype):
    if stack is None:
        return jax.ShapeDtypeStruct(shape, dtype), [], [], lambda n_in, n_out: {}
    return (jax.ShapeDtypeStruct(stack.shape, stack.dtype), [stack],
            [pl.BlockSpec(memory_space=pl.ANY)], lambda n_in, n_out: {n_in: n_out})


def _rmsnorm_kernel(x_ref, g_ref, o_ref):
    x = x_ref[...]
    y = x * lax.rsqrt(jnp.mean(x * x, axis=-1, keepdims=True) + EPS)
    o_ref[...] = (y * g_ref[...]).astype(o_ref.dtype)


def _rmsnorm(x, g, out_dtype):
    m, d = x.shape
    tm = next((c for c in (256, 536) if m % c == 0), m)
    return pl.pallas_call(
        _rmsnorm_kernel,
        grid=(m // tm,),
        in_specs=[pl.BlockSpec((tm, d), lambda i: (i, 0)),
                  pl.BlockSpec((1, d), lambda i: (0, 0))],
        out_specs=pl.BlockSpec((tm, d), lambda i: (i, 0)),
        out_shape=jax.ShapeDtypeStruct((m, d), out_dtype),
        compiler_params=_params(("parallel",)),
        name="rmsnorm",
    )(x, g.reshape(1, d))


def _final_norm_prompt(h, g, b, t, seq):
    d = h.shape[1]
    skip = (t - seq) // CHUNK
    return pl.pallas_call(
        _rmsnorm_kernel,
        grid=(b, seq // CHUNK),
        in_specs=[pl.BlockSpec((CHUNK, d), lambda i, j: (i * (t // CHUNK) + j + skip, 0)),
                  pl.BlockSpec((1, d), lambda i, j: (0, 0))],
        out_specs=pl.BlockSpec((None, CHUNK, d), lambda i, j: (i, j, 0)),
        out_shape=jax.ShapeDtypeStruct((b, seq, d), h.dtype),
        compiler_params=_params(("parallel", "parallel")),
        name="final_norm",
    )(h, g.reshape(1, d))


MM_K = 4096
MM_TN = 512


def _mm_kernel(*refs, epilogue, m_axis):
    x_ref, w_ref = refs[0], refs[1]
    o_ref, w16_ref = refs[-2], refs[-1]

    @pl.when(pl.program_id(m_axis) == 0)
    def _():
        w16_ref[...] = w_ref[...].astype(bf16)

    acc = jnp.dot(x_ref[...], w16_ref[...], preferred_element_type=f32)
    if epilogue == "plain":
        o_ref[...] = acc.astype(o_ref.dtype)
    elif epilogue == "gate":
        scale_ref, z_ref = refs[2], refs[3]
        o_ref[...] = (acc * scale_ref[...] * _silu(z_ref[...])).astype(o_ref.dtype)
    elif epilogue == "residual":
        h_ref = refs[2]
        o_ref[...] = h_ref[...] + acc
    else:
        raise ValueError(epilogue)


def _mm_rows(m, epilogue):
    for tm in ((1072, 1056) if epilogue == "residual" else (1408, 1072, 1056)):
        if m % tm == 0:
            return tm
    return m


def _matmul(x, w, layer, col0, ncols, kblock=0, out_dtype=f32, residual=None):
    m = x.shape[0]
    tk = min(MM_K, x.shape[1])
    epilogue = "plain" if residual is None else "residual"
    tm = _mm_rows(m, epilogue)
    tn = min(MM_TN, ncols)
    off = col0 // tn
    assert col0 % tn == 0 and ncols % tn == 0 and x.shape[1] % tk == 0
    in_specs = [pl.BlockSpec((tm, tk), lambda j, i: (i, kblock)),
                pl.BlockSpec((None, tk, tn), lambda j, i: (layer, kblock, j + off))]
    args = [x, w]
    if residual is not None:
        in_specs.append(pl.BlockSpec((tm, tn), lambda j, i: (i, j)))
        args.append(residual)
    return pl.pallas_call(
        functools.partial(_mm_kernel, epilogue=epilogue, m_axis=1),
        grid=(ncols // tn, m // tm),
        in_specs=in_specs,
        out_specs=pl.BlockSpec((tm, tn), lambda j, i: (i, j)),
        out_shape=jax.ShapeDtypeStruct((m, ncols), out_dtype),
        scratch_shapes=[pltpu.VMEM((tk, tn), bf16)],
        compiler_params=_params(("parallel", "arbitrary")),
        name="matmul",
    )(*args)


def _out_project(h, act, w_out, layer):
    tk = min(MM_K, act.shape[1])
    for kb in range(act.shape[1] // tk):
        h = _matmul(act, w_out, layer, 0, w_out.shape[2], kblock=kb, residual=h)
    return h


def _grouped_matmul_gate(d, w_grp, layer, scale, z):
    m, e = d.shape
    _, ng, gk, gn = w_grp.shape
    tm = _mm_rows(m, "gate")
    tn = min(2 * MM_TN, gn)
    nb = gn // tn
    return pl.pallas_call(
        functools.partial(_mm_kernel, epilogue="gate", m_axis=2),
        grid=(ng, nb, m // tm),
        in_specs=[pl.BlockSpec((tm, gk), lambda g, j, i: (i, g)),
                  pl.BlockSpec((None, None, gk, tn), lambda g, j, i: (layer, g, 0, j)),
                  pl.BlockSpec((1, tn), lambda g, j, i: (0, g * nb + j)),
                  pl.BlockSpec((tm, tn), lambda g, j, i: (i, g * nb + j))],
        out_specs=pl.BlockSpec((tm, tn), lambda g, j, i: (i, g * nb + j)),
        out_shape=jax.ShapeDtypeStruct((m, e), bf16),
        scratch_shapes=[pltpu.VMEM((gk, tn), bf16)],
        compiler_params=_params(("parallel", "parallel", "arbitrary")),
        name="grouped_matmul_gate",
    )(d, w_grp, scale.reshape(1, e), z)


POOL_HALO = 16
POOL_ROWS = 16
POOL_LANES = 512


def _row_pieces(x):
    return [x[r:r + SUBLANES] for r in range(0, x.shape[0], SUBLANES)]


def _shift_rows(pieces, d):
    if d == SUBLANES:
        return [pieces[0]] + pieces[:-1]
    rolled = [pltpu.roll(p, d, 0) for p in pieces]
    keep = lax.broadcasted_iota(jnp.int32, pieces[0].shape, 0) >= d
    return [rolled[0]] + [jnp.where(keep, rolled[k], rolled[k - 1]) for k in range(1, len(pieces))]


def _window_sums(pieces, w):
    d = 1
    while d < w:
        pieces = [a + b for a, b in zip(pieces, _shift_rows(pieces, d))]
        d *= 2
    return pieces


def _pool_diff_prompt_kernel(halo_ref, cur_ref, o_ref, *, tm, gsz):
    t = pl.program_id(1)
    gi = (pl.program_id(2) * POOL_LANES) // gsz
    for widx, w in enumerate(POOL_WINDOWS):
        ctx = -(-(w - 1) // SUBLANES) * SUBLANES

        @pl.when(gi == widx)
        def _(w=w, ctx=ctx):
            for r in range(0, tm, POOL_ROWS):
                cur = cur_ref[r:r + POOL_ROWS, :]
                if r == 0:
                    prev = jnp.where(t > 0, halo_ref[POOL_HALO - ctx:POOL_HALO, :], 0.0)
                else:
                    prev = cur_ref[r - ctx:r, :]
                s = _window_sums(_row_pieces(prev) + _row_pieces(cur), w)[ctx // SUBLANES:]
                s = jnp.concatenate(s, axis=0)
                pos = lax.broadcasted_iota(jnp.int32, (POOL_ROWS, 1), 0) + (t * tm + r - FRONT)
                cnt = jnp.clip(pos + 1, 1, w).astype(f32)
                o_ref[r:r + POOL_ROWS, :] = (s / cnt - cur).astype(o_ref.dtype)


def _pool_diff_prompt(u, b, t):
    m, e = u.shape
    tm = 1056
    assert t % tm == 0 and tm % POOL_HALO == 0 and e % (len(POOL_WINDOWS) * POOL_LANES) == 0
    hb = tm // POOL_HALO
    nt = t // tm
    return pl.pallas_call(
        functools.partial(_pool_diff_prompt_kernel, tm=tm, gsz=e // len(POOL_WINDOWS)),
        grid=(b, nt, e // POOL_LANES),
        in_specs=[pl.BlockSpec((POOL_HALO, POOL_LANES),
                               lambda i, j, c: (jnp.maximum((i * nt + j) * hb - 1, 0), c)),
                  pl.BlockSpec((tm, POOL_LANES), lambda i, j, c: (i * nt + j, c))],
        out_specs=pl.BlockSpec((tm, POOL_LANES), lambda i, j, c: (i * nt + j, c)),
        out_shape=jax.ShapeDtypeStruct((m, e), bf16),
        compiler_params=_params(("parallel", "parallel", "parallel")),
        name="pool_diff_prompt",
    )(u, u)


def _pool_diff_sample_kernel(*refs, gsz, lanes):
    st_ref, u_ref = refs[0], refs[1]
    o_ref, ns_ref = refs[-2], refs[-1]
    gi = (pl.program_id(1) * lanes) // gsz
    for widx, w in enumerate(POOL_WINDOWS):
        @pl.when(gi == widx)
        def _(w=w):
            cur = u_ref[...]
            s = cur
            for i in range(1, w):
                s = s + st_ref[:, POOL_CTX - i, :]
            cnt = float(min(PAST_LEN + 1, w))
            o_ref[...] = (s / cnt - cur).astype(o_ref.dtype)
    for r in range(POOL_CTX - 1):
        ns_ref[:, r, :] = st_ref[:, r + 1, :]
    ns_ref[:, POOL_CTX - 1, :] = u_ref[...]


def _pool_diff_sample(states, layer, u, d, row0, new_states):
    _, b, _, e = states.shape
    gsz = e // len(POOL_WINDOWS)
    bb, lanes = 16, min(1024, gsz)
    assert b % bb == 0 and gsz % lanes == 0 and row0 % bb == 0
    st_spec = pl.BlockSpec((None, bb, POOL_CTX, lanes), lambda i, c: (layer, i, 0, c))
    row_spec = pl.BlockSpec((bb, lanes), lambda i, c: (row0 // bb + i, c))
    ns_shape, extra, extra_specs, alias = _stacked_out(new_states, layer, states.shape, states.dtype)
    return pl.pallas_call(
        functools.partial(_pool_diff_sample_kernel, gsz=gsz, lanes=lanes),
        grid=(b // bb, e // lanes),
        in_specs=[st_spec, row_spec, pl.BlockSpec(memory_space=pl.ANY)] + extra_specs,
        out_specs=[row_spec, st_spec],
        out_shape=[jax.ShapeDtypeStruct(d.shape, d.dtype), ns_shape],
        input_output_aliases={2: 0, **alias(3, 1)},
        compiler_params=_params(("parallel", "parallel")),
        name="pool_diff_sample",
    )(states, u, d, *extra)


CONV_HALO = 8
CONV_ROWS = 16
CONV_LANES = 1024


def _conv_act(acc, scale):
    y = _silu(acc)
    if scale is None:
        return y
    parts = []
    for h in range(y.shape[1] // HEAD_DIM):
        yh = y[:, h * HEAD_DIM:(h + 1) * HEAD_DIM]
        parts.append(yh * lax.rsqrt(jnp.sum(yh * yh, axis=-1, keepdims=True) + EPS) * scale)
    return jnp.concatenate(parts, axis=1)


def _conv_branches(c, key_dim, body):
    normalise = c * CONV_LANES < 2 * key_dim
    q_scale = jnp.where(c * CONV_LANES < key_dim, HEAD_DIM ** -0.5, 1.0)
    pl.when(normalise)(lambda: body(q_scale))
    pl.when(jnp.logical_not(normalise))(lambda: body(None))


def _conv_prompt_kernel(halo_ref, cur_ref, w_ref, o_ref, *, tm, key_dim):
    t = pl.program_id(1)

    def body(scale):
        for r in range(0, tm, CONV_ROWS):
            if r == 0:
                prev = jnp.where(t > 0, halo_ref[...], 0.0)
            else:
                prev = cur_ref[r - CONV_HALO:r, :]
            cur = cur_ref[r:r + CONV_ROWS, :]
            pieces = _row_pieces(prev) + _row_pieces(cur)
            acc = cur * w_ref[CONV_WIDTH - 1:CONV_WIDTH, :]
            for d in range(1, CONV_WIDTH):
                tap = CONV_WIDTH - 1 - d
                back = jnp.concatenate(_shift_rows(pieces, d)[CONV_HALO // SUBLANES:], axis=0)
                acc = acc + back * w_ref[tap:tap + 1, :]
            o_ref[r:r + CONV_ROWS, :] = _conv_act(acc, scale)

    _conv_branches(pl.program_id(2), key_dim, body)


def _conv_prompt(x, conv_w, key_dim, b, t):
    m, c = x.shape
    tm = 1056
    assert t % tm == 0 and tm % CONV_ROWS == 0 and tm % CONV_HALO == 0 and key_dim % CONV_LANES == 0
    hb = tm // CONV_HALO
    nt = t // tm
    return pl.pallas_call(
        functools.partial(_conv_prompt_kernel, tm=tm, key_dim=key_dim),
        grid=(b, nt, c // CONV_LANES),
        in_specs=[pl.BlockSpec((CONV_HALO, CONV_LANES),
                               lambda i, j, l: (jnp.maximum((i * nt + j) * hb - 1, 0), l)),
                  pl.BlockSpec((tm, CONV_LANES), lambda i, j, l: (i * nt + j, l)),
                  pl.BlockSpec((CONV_WIDTH, CONV_LANES), lambda i, j, l: (0, l))],
        out_specs=pl.BlockSpec((tm, CONV_LANES), lambda i, j, l: (i * nt + j, l)),
        out_shape=jax.ShapeDtypeStruct((m, c), f32),
        compiler_params=_params(("parallel", "parallel", "parallel")),
        name="conv_prompt",
    )(x, x, conv_w)


def _conv_sample_kernel(*refs, key_dim):
    st_ref, x_ref, w_ref = refs[0], refs[1], refs[2]
    o_ref, ns_ref = refs[-2], refs[-1]

    def body(scale):
        acc = x_ref[...] * w_ref[CONV_WIDTH - 1:CONV_WIDTH, :]
        for i in range(CONV_WIDTH - 1):
            acc = acc + st_ref[:, i, :] * w_ref[i:i + 1, :]
        o_ref[...] = _conv_act(acc, scale)

    _conv_branches(pl.program_id(0), key_dim, body)
    for r in range(CONV_WIDTH - 2):
        ns_ref[:, r, :] = st_ref[:, r + 1, :]
    ns_ref[:, CONV_WIDTH - 2, :] = x_ref[...]


def _conv_sample(states, layer, x, act, row0, conv_w, key_dim, new_states):
    b = states.shape[1]
    c = x.shape[1]
    assert row0 % b == 0
    st_spec = pl.BlockSpec((None, b, CONV_WIDTH - 1, CONV_LANES), lambda l: (layer, 0, 0, l))
    row_spec = pl.BlockSpec((b, CONV_LANES), lambda l: (row0 // b, l))
    ns_shape, extra, extra_specs, alias = _stacked_out(new_states, layer, states.shape, states.dtype)
    return pl.pallas_call(
        functools.partial(_conv_sample_kernel, key_dim=key_dim),
        grid=(c // CONV_LANES,),
        in_specs=[st_spec, row_spec,
                  pl.BlockSpec((CONV_WIDTH, CONV_LANES), lambda l: (0, l)),
                  pl.BlockSpec(memory_space=pl.ANY)] + extra_specs,
        out_specs=[row_spec, st_spec],
        out_shape=[jax.ShapeDtypeStruct(act.shape, act.dtype), ns_shape],
        input_output_aliases={3: 0, **alias(4, 1)},
        compiler_params=_params(("parallel",)),
        name="conv_sample",
    )(states, x, conv_w, act, *extra)


def _gate_values(ba_ref, a_ref, dt_ref):
    x = ba_ref[...]
    xa = x + dt_ref[...]
    softplus = jnp.maximum(xa, 0.0) + jnp.log1p(jnp.exp(-jnp.abs(xa)))
    return jax.nn.sigmoid(x), -jnp.exp(a_ref[...]) * softplus


def _gate_chunk_kernel(ba_ref, a_ref, dt_ref, o_ref, *, n_heads):
    beta, g = _gate_values(ba_ref, a_ref, dt_ref)
    row = lax.broadcasted_iota(jnp.int32, (CHUNK, 1), 0)
    g = jnp.where(row < jnp.where(pl.program_id(1) == 0, FRONT, 0), 0.0, g)
    shift = 1
    while shift < CHUNK:
        g = g + jnp.where(row >= shift, pltpu.roll(g, shift, 0), 0.0)
        shift *= 2
    lane = lax.broadcasted_iota(jnp.int32, g.shape, 1)
    o_ref[...] = jnp.where(lane < n_heads, beta, g)


def _gates_chunked(ba, a_pad, dt_pad, b, t):
    w = ba.shape[1]
    nc = t // CHUNK
    return pl.pallas_call(
        functools.partial(_gate_chunk_kernel, n_heads=w // 2),
        grid=(b, nc),
        in_specs=[pl.BlockSpec((CHUNK, w), lambda i, j: (i * nc + j, 0)),
                  pl.BlockSpec((1, w), lambda i, j: (0, 0)),
                  pl.BlockSpec((1, w), lambda i, j: (0, 0))],
        out_specs=pl.BlockSpec((CHUNK, w), lambda i, j: (i * nc + j, 0)),
        out_shape=jax.ShapeDtypeStruct((b * t, w), f32),
        compiler_params=_params(("parallel", "parallel")),
        name="gdn_gates_chunked",
    )(ba, a_pad, dt_pad)


def _gate_step_kernel(ba_ref, a_ref, dt_ref, beta_ref, g_ref, *, n_heads):
    beta, g = _gate_values(ba_ref, a_ref, dt_ref)
    rows = beta.shape[0]
    for h in range(n_heads):
        sl = slice(h * HEAD_DIM, (h + 1) * HEAD_DIM)
        beta_ref[:, sl] = jnp.broadcast_to(beta[:, h:h + 1], (rows, HEAD_DIM))
        g_ref[:, sl] = jnp.broadcast_to(g[:, n_heads + h:n_heads + h + 1], (rows, HEAD_DIM))


def _gates_step(ba, a_pad, dt_pad, row0, b):
    w = ba.shape[1]
    n_heads = w // 2
    wide = n_heads * HEAD_DIM
    assert row0 % b == 0
    return pl.pallas_call(
        functools.partial(_gate_step_kernel, n_heads=n_heads),
        grid=(1,),
        in_specs=[pl.BlockSpec((b, w), lambda i: (row0 // b, 0)),
                  pl.BlockSpec((1, w), lambda i: (0, 0)),
                  pl.BlockSpec((1, w), lambda i: (0, 0))],
        out_specs=[pl.BlockSpec((b, wide), lambda i: (0, 0)),
                   pl.BlockSpec((b, wide), lambda i: (0, 0))],
        out_shape=[jax.ShapeDtypeStruct((b, wide), f32), jax.ShapeDtypeStruct((b, wide), f32)],
        compiler_params=_params(("arbitrary",)),
        name="gdn_gates_step",
    )(ba, a_pad, dt_pad)


def _gate_params(a_log, dt_bias):
    zeros = jnp.zeros(a_log.shape, f32)
    w = 2 * a_log.shape[0]
    return (jnp.concatenate([zeros, a_log.astype(f32)]).reshape(1, w),
            jnp.concatenate([zeros, dt_bias.astype(f32)]).reshape(1, w))


def _dot_nt(a, b):
    return lax.dot_general(a, b, (((1,), (1,)), ((), ())), preferred_element_type=f32)


def _dot_tn(a, b):
    return lax.dot_general(a, b, (((0,), (0,)), ((), ())), preferred_element_type=f32)


def _dot16(a, b):
    return jnp.dot(a.astype(bf16), b.astype(bf16), preferred_element_type=f32)


def _stack(x, idx):
    return jnp.concatenate([x[:, i * HEAD_DIM:(i + 1) * HEAD_DIM] for i in idx], axis=0)


def _gdn_chunk_kernel(q_ref, k_ref, v_ref, bg_ref, gcr_ref, z_ref, ng_ref,
                      o_ref, sout_ref, s_ref):
    c = pl.program_id(2)

    @pl.when(c == 0)
    def _():
        s_ref[...] = jnp.zeros_like(s_ref)

    quads = range(QUADS_PER_STEP)
    heads = tuple(range(QUAD))
    khead = tuple(h // 2 for h in heads)
    rows = QUAD * CHUNK
    kq_w = QUAD // 2 * HEAD_DIM
    v_w = QUAD * HEAD_DIM

    def head_rows(h):
        return slice(h * CHUNK, (h + 1) * CHUNK)

    ri = lax.broadcasted_iota(jnp.int32, (rows, rows), 0)
    ci = lax.broadcasted_iota(jnp.int32, (rows, rows), 1)
    shift = CHUNK.bit_length() - 1
    same = lax.shift_right_logical(ri, shift) == lax.shift_right_logical(ci, shift)
    lower = same & (ri >= ci)
    strict = ri > ci

    bg = bg_ref[...]
    n_step = QUAD * QUADS_PER_STEP
    k16, kb16, q16, rhs, qd, kd16, decay, g_last = [], [], [], [], [], [], [], []
    for i in quads:
        kq = slice(i * kq_w, (i + 1) * kq_w)
        vs = slice(i * v_w, (i + 1) * v_w)
        kst = _stack(k_ref[:, kq], khead)
        qst = _stack(q_ref[:, kq], khead)
        cols = [i * QUAD + h for h in heads]
        bcol = [jnp.broadcast_to(bg[:, n:n + 1], (CHUNK, HEAD_DIM)) for n in cols]
        gcol = [jnp.broadcast_to(bg[:, n_step + n:n_step + n + 1], (CHUNK, HEAD_DIM)) for n in cols]
        bst = jnp.concatenate(bcol, axis=0)
        gst = jnp.concatenate(gcol, axis=0)
        last = [g[CHUNK - 1:CHUNK] for g in gcol]
        glast = jnp.concatenate([jnp.broadcast_to(r, (CHUNK, HEAD_DIM)) for r in last], axis=0)
        eg = jnp.exp(gst)
        kb = kst * bst
        k16.append(kst.astype(bf16))
        kb16.append(kb.astype(bf16))
        q16.append(qst.astype(bf16))
        rhs.append(jnp.concatenate([_stack(v_ref[:, vs], heads) * bst, kb * eg], axis=1))
        qd.append(qst * eg)
        kd16.append((kst * jnp.exp(glast - gst)).astype(bf16))
        diff = jnp.concatenate([gst] * (rows // HEAD_DIM), axis=1) - gcr_ref[i]
        decay.append(jnp.exp(jnp.where(lower, diff, -jnp.inf)))
        g_last.append([jnp.exp(r) for r in last])

    pw = [-jnp.where(strict, _dot_nt(kb16[i], k16[i]) * decay[i], 0.0) for i in quads]
    attn16 = [(_dot_nt(q16[i], k16[i]) * decay[i]).astype(bf16) for i in quads]

    sol = [rhs[i] + _dot16(pw[i], rhs[i]) for i in quads]
    for _ in range(NEUMANN_LEVELS - 1):
        pw16 = [p.astype(bf16) for p in pw]
        pw = [jnp.dot(p, p, preferred_element_type=f32) for p in pw16]
        sol = [sol[i] + _dot16(pw[i], sol[i]) for i in quads]

    x = [[_dot16(jnp.concatenate([sol[i][head_rows(h), HEAD_DIM:], qd[i][head_rows(h)]], axis=0),
                 s_ref[i * QUAD + h]) for h in heads] for i in quads]
    vn16 = [jnp.concatenate([sol[i][head_rows(h), :HEAD_DIM] - x[i][h][:CHUNK] for h in heads],
                            axis=0).astype(bf16) for i in quads]
    o = [jnp.concatenate([x[i][h][CHUNK:] for h in heads], axis=0)
         + jnp.dot(attn16[i], vn16[i], preferred_element_type=f32) for i in quads]
    for i in quads:
        for h in heads:
            s_ref[i * QUAD + h] = (s_ref[i * QUAD + h] * g_last[i][h]
                                   + _dot_tn(kd16[i][head_rows(h)], vn16[i][head_rows(h)]))

    for i in quads:
        on = o[i] * lax.rsqrt(jnp.mean(o[i] * o[i], axis=-1, keepdims=True) + EPS) * ng_ref[...]
        for h in heads:
            sl = slice(i * v_w + h * HEAD_DIM, i * v_w + (h + 1) * HEAD_DIM)
            o_ref[:, sl] = (on[head_rows(h)] * _silu(z_ref[:, sl])).astype(o_ref.dtype)

    @pl.when(c == pl.num_programs(2) - 1)
    def _():
        sout_ref[...] = s_ref[...]


def _gdn_chunked(qkv, bg, gc_rows, z, norm_g, batch):
    m, value_dim = z.shape
    key_dim = (qkv.shape[1] - value_dim) // 2
    n_heads = value_dim // HEAD_DIM
    n_chunks = bg.shape[1] // batch // CHUNK
    heads_per_step = QUAD * QUADS_PER_STEP
    n_groups = n_heads // heads_per_step
    kq_w = heads_per_step // 2 * HEAD_DIM
    v_w = heads_per_step * HEAD_DIM

    def row_spec(width, col0=0):
        off = col0 // width
        return pl.BlockSpec((CHUNK, width), lambda b, g, c: (b * n_chunks + c, g + off))

    return pl.pallas_call(
        _gdn_chunk_kernel,
        grid=(batch, n_groups, n_chunks),
        in_specs=[row_spec(kq_w), row_spec(kq_w, key_dim), row_spec(v_w, 2 * key_dim),
                  pl.BlockSpec((None, CHUNK, 2 * heads_per_step),
                               lambda b, g, c: (g, b * n_chunks + c, 0)),
                  pl.BlockSpec((None, None, QUADS_PER_STEP, 1, QUAD * CHUNK),
                               lambda b, g, c: (b, c, g, 0, 0)),
                  row_spec(v_w),
                  pl.BlockSpec((1, HEAD_DIM), lambda b, g, c: (0, 0))],
        out_specs=[row_spec(v_w),
                   pl.BlockSpec((None, heads_per_step, HEAD_DIM, HEAD_DIM),
                                lambda b, g, c: (b, g, 0, 0))],
        out_shape=[jax.ShapeDtypeStruct((m, value_dim), bf16),
                   jax.ShapeDtypeStruct((batch, n_heads, HEAD_DIM, HEAD_DIM), f32)],
        scratch_shapes=[pltpu.VMEM((heads_per_step, HEAD_DIM, HEAD_DIM), f32)],
        compiler_params=_params(("parallel", "parallel", "arbitrary")),
        name="gdn_chunked",
    )(qkv, qkv, qkv, bg, gc_rows, z, norm_g.reshape(1, HEAD_DIM))


def _gdn_step_kernel(*refs, n_heads):
    qt_ref, kt_ref, v_ref, beta_ref, g_ref, z_ref, ng_ref, s_ref = refs[:8]
    o_ref, sout_ref, kcol_ref, qcol_ref, eg_ref, oraw_ref = refs[-6:]
    n_k = qt_ref.shape[1]
    rep = n_heads // n_k
    for kh in range(n_k):
        kcol_ref[kh] = jnp.broadcast_to(kt_ref[:, kh:kh + 1], (HEAD_DIM, HEAD_DIM))
        qcol_ref[kh] = jnp.broadcast_to(qt_ref[:, kh:kh + 1], (HEAD_DIM, HEAD_DIM))
    eg_ref[...] = jnp.exp(g_ref[...])
    for h in range(n_heads):
        kcol = kcol_ref[h // rep]
        s = s_ref[h] * eg_ref[h:h + 1, :]
        kv = jnp.sum(kcol * s, axis=0, keepdims=True)
        delta = (v_ref[h:h + 1, :] - kv) * beta_ref[h:h + 1, :]
        s = s + kcol * delta
        sout_ref[h] = s
        oraw_ref[h:h + 1, :] = jnp.sum(qcol_ref[h // rep] * s, axis=0, keepdims=True)
    o = oraw_ref[...]
    on = o * lax.rsqrt(jnp.mean(o * o, axis=-1, keepdims=True) + EPS) * ng_ref[...]
    o_ref[...] = on * _silu(z_ref[...])


def _gdn_step(qt, kt, v3, beta3, g3, z3, norm_g, states, layer, new_states):
    b, n_heads, _ = v3.shape
    n_k = qt.shape[2]

    def head_spec():
        return pl.BlockSpec((None, n_heads, HEAD_DIM), lambda i: (i, 0, 0))

    def t_spec():
        return pl.BlockSpec((None, HEAD_DIM, n_k), lambda i: (i, 0, 0))

    s_spec = pl.BlockSpec((None, None, n_heads, HEAD_DIM, HEAD_DIM),
                          lambda i: (layer, i, 0, 0, 0))
    ns_shape, extra, extra_specs, alias = _stacked_out(new_states, layer, states.shape, f32)
    return pl.pallas_call(
        functools.partial(_gdn_step_kernel, n_heads=n_heads),
        grid=(b,),
        in_specs=[t_spec(), t_spec(), head_spec(), head_spec(), head_spec(), head_spec(),
                  pl.BlockSpec((1, HEAD_DIM), lambda i: (0, 0)), s_spec] + extra_specs,
        out_specs=[head_spec(), s_spec],
        out_shape=[jax.ShapeDtypeStruct((b, n_heads, HEAD_DIM), f32), ns_shape],
        scratch_shapes=[pltpu.VMEM((n_k, HEAD_DIM, HEAD_DIM), f32),
                        pltpu.VMEM((n_k, HEAD_DIM, HEAD_DIM), f32),
                        pltpu.VMEM((n_heads, HEAD_DIM), f32),
                        pltpu.VMEM((n_heads, HEAD_DIM), f32)],
        input_output_aliases=alias(8, 1),
        compiler_params=_params(("parallel",)),
        name="gdn_step",
    )(qt, kt, v3, beta3, g3, z3, norm_g.reshape(1, HEAD_DIM), states, *extra)


def _pool_layer(h, rows_p, states, layer, new_states, w_in, w_grp, scale, w_out, norm_g, batch):
    e = w_grp.shape[1] * w_grp.shape[2]
    t = rows_p // batch
    xn = _rmsnorm(h, norm_g, bf16)
    u = _matmul(xn, w_in, layer, 0, e)
    z = _matmul(xn, w_in, layer, e, e)
    d = _pool_diff_prompt(u, batch, t)
    d, new_states = _pool_diff_sample(states, layer, u, d, rows_p, new_states)
    h = _out_project(h, _grouped_matmul_gate(d, w_grp, layer, scale, z), w_out, layer)
    pool_p = u[:rows_p].reshape(batch, t, e)[:, -POOL_CTX:]
    return h, pool_p, new_states


def _gdn_layer(h, rows_p, conv_states, ssm_states, layer, new_conv, new_ssm, w_in, conv_w, a_log,
               dt_bias, head_g, w_out, norm_g, batch):
    value_dim = w_out.shape[1]
    n_heads = value_dim // HEAD_DIM
    conv_dim = conv_w.shape[1]
    key_dim = (conv_dim - value_dim) // 2
    n_k = key_dim // HEAD_DIM
    t = rows_p // batch
    bs = h.shape[0] - rows_p

    xn = _rmsnorm(h, norm_g, bf16)
    qkv = _matmul(xn, w_in, layer, 0, conv_dim)
    z = _matmul(xn, w_in, layer, conv_dim, value_dim)
    ba = _matmul(xn, w_in, layer, conv_dim + value_dim, 2 * n_heads)
    act = _conv_prompt(qkv, conv_w, key_dim, batch, t)
    act, new_conv = _conv_sample(conv_states, layer, qkv, act, rows_p, conv_w, key_dim, new_conv)
    a_pad, dt_pad = _gate_params(a_log, dt_bias)

    bg = _gates_chunked(ba, a_pad, dt_pad, batch, t)
    gc = bg[:, n_heads:]
    gc_rows = gc.reshape(batch, t // CHUNK, CHUNK, n_heads // QUAD, QUAD)
    gc_rows = jnp.transpose(gc_rows, (0, 1, 3, 4, 2)).reshape(
        batch, t // CHUNK, n_heads // QUAD, 1, QUAD * CHUNK)
    n_step = QUAD * QUADS_PER_STEP
    bg = jnp.concatenate([bg[:, :n_heads].reshape(-1, n_heads // n_step, n_step),
                          gc.reshape(-1, n_heads // n_step, n_step)], axis=2)
    o, ssm_p = _gdn_chunked(act, jnp.transpose(bg, (1, 0, 2)), gc_rows, z, head_g, batch)
    conv_p = qkv[:rows_p].reshape(batch, t, conv_dim)[:, -(CONV_WIDTH - 1):]

    beta_b, g_b = _gates_step(ba, a_pad, dt_pad, rows_p, bs)
    act_s = act[rows_p:]
    qt = jnp.transpose(act_s[:, :key_dim].reshape(bs, n_k, HEAD_DIM), (0, 2, 1))
    kt = jnp.transpose(act_s[:, key_dim:2 * key_dim].reshape(bs, n_k, HEAD_DIM), (0, 2, 1))
    per_head = (bs, n_heads, HEAD_DIM)
    o_s, new_ssm = _gdn_step(qt, kt, act_s[:, 2 * key_dim:].reshape(per_head),
                             beta_b.reshape(per_head), g_b.reshape(per_head),
                             z[rows_p:].reshape(per_head), head_g, ssm_states, layer, new_ssm)
    o = lax.dynamic_update_slice(o, o_s.reshape(bs, value_dim).astype(bf16), (rows_p, 0))
    h = _out_project(h, o, w_out, layer)
    return h, conv_p, ssm_p, new_conv, new_ssm


def kernel(x_prompt, x_sample, state_pool, state_conv, state_ssm, meta_tokens, norm_g, final_norm_g, pool_w_in, pool_w_grp, pool_scale, pool_w_out, gdn_w_in, gdn_conv_w, gdn_A_log, gdn_dt_bias, gdn_norm_g, gdn_w_out):
    dt = x_prompt.dtype
    batch, seq, d = x_prompt.shape
    depth = norm_g.shape[0]
    meta = jnp.broadcast_to(meta_tokens.astype(dt)[None], (batch, N_META, d))
    hp = jnp.concatenate([jnp.zeros((batch, FRONT, d), dt), meta, x_prompt], axis=1)
    t = hp.shape[1]
    rows_p = batch * t
    h = jnp.concatenate([hp.reshape(rows_p, d), x_sample.reshape(x_sample.shape[0], d)], axis=0)
    pool_p, conv_p, ssm_p = [], [], []
    pool_s = conv_s = ssm_s = None
    for i in range(depth):
        j = i // 2
        if i % 2 == 0:
            h, pp, pool_s = _pool_layer(h, rows_p, state_pool, j, pool_s, pool_w_in, pool_w_grp,
                                        pool_scale[j], pool_w_out, norm_g[i], batch)
            pool_p.append(pp)
        else:
            h, cp, sp, conv_s, ssm_s = _gdn_layer(
                h, rows_p, state_conv, state_ssm, j, conv_s, ssm_s, gdn_w_in, gdn_conv_w[j],
                gdn_A_log[j], gdn_dt_bias[j], gdn_norm_g[j], gdn_w_out, norm_g[i], batch)
            conv_p.append(cp)
            ssm_p.append(sp)
    y_prompt = _final_norm_prompt(h, final_norm_g, batch, t, seq)
    y_sample = _rmsnorm(h[rows_p:], final_norm_g, dt).reshape(x_sample.shape)
    return (y_prompt, y_sample, jnp.stack(pool_p), jnp.stack(conv_p), jnp.stack(ssm_p),
            pool_s, conv_s, ssm_s)
```

```python
import functools

import jax
import jax.numpy as jnp
from jax import lax
from jax.experimental import pallas as pl
from jax.experimental.pallas import tpu as pltpu

N_META = 16
PAST_LEN = 16384
POOL_WINDOWS = (2, 4, 8, 16)
POOL_CTX = max(POOL_WINDOWS) - 1
HEAD_DIM = 128
CONV_WIDTH = 4
CHUNK = 64
EPS = 1e-6
FRONT = (-N_META) % CHUNK
QUAD = 4
QUADS_PER_STEP = 8
NEUMANN_LEVELS = 6

VMEM_LIMIT = 56 * 1024 * 1024
SUBLANES = 8

f32 = jnp.float32
bf16 = jnp.bfloat16


def _params(sem):
    return pltpu.CompilerParams(dimension_semantics=sem, vmem_limit_bytes=VMEM_LIMIT)


def _silu(x):
    return x * jax.nn.sigmoid(x)


def _stacked_out(stack, layer, shape, dtype):
    if stack is None:
        return jax.ShapeDtypeStruct(shape, dtype), [], [], lambda n_in, n_out: {}
    return (jax.ShapeDtypeStruct(stack.shape, stack.dtype), [stack],
            [pl.BlockSpec(memory_space=pl.ANY)], lambda n_in, n_out: {n_in: n_out})


def _rmsnorm_kernel(x_ref, g_ref, o_ref):
    x = x_ref[...]
    y = x * lax.rsqrt(jnp.mean(x * x, axis=-1, keepdims=True) + EPS)
    o_ref[...] = (y * g_ref[...]).astype(o_ref.dtype)


def _rmsnorm(x, g, out_dtype):
    m, d = x.shape
    tm = next((c for c in (256, 536) if m % c == 0), m)
    return pl.pallas_call(
        _rmsnorm_kernel,
        grid=(m // tm,),
        in_specs=[pl.BlockSpec((tm, d), lambda i: (i, 0)),
                  pl.BlockSpec((1, d), lambda i: (0, 0))],
        out_specs=pl.BlockSpec((tm, d), lambda i: (i, 0)),
        out_shape=jax.ShapeDtypeStruct((m, d), out_dtype),
        compiler_params=_params(("parallel",)),
        name="rmsnorm",
    )(x, g.reshape(1, d))


def _final_norm_prompt(h, g, b, t, seq):
    d = h.shape[1]
    skip = (t - seq) // CHUNK
    return pl.pallas_call(
        _rmsnorm_kernel,
        grid=(b, seq // CHUNK),
        in_specs=[pl.BlockSpec((CHUNK, d), lambda i, j: (i * (t // CHUNK) + j + skip, 0)),
                  pl.BlockSpec((1, d), lambda i, j: (0, 0))],
        out_specs=pl.BlockSpec((None, CHUNK, d), lambda i, j: (i, j, 0)),
        out_shape=jax.ShapeDtypeStruct((b, seq, d), h.dtype),
        compiler_params=_params(("parallel", "parallel")),
        name="final_norm",
    )(h, g.reshape(1, d))


MM_K = 4096
MM_TN = 512


def _mm_kernel(*refs, epilogue, m_axis):
    x_ref, w_ref = refs[0], refs[1]
    o_ref, w16_ref = refs[-2], refs[-1]

    @pl.when(pl.program_id(m_axis) == 0)
    def _():
        w16_ref[...] = w_ref[...].astype(bf16)

    acc = jnp.dot(x_ref[...], w16_ref[...], preferred_element_type=f32)
    if epilogue == "plain":
        o_ref[...] = acc.astype(o_ref.dtype)
    elif epilogue == "gate":
        scale_ref, z_ref = refs[2], refs[3]
        o_ref[...] = (acc * scale_ref[...] * _silu(z_ref[...])).astype(o_ref.dtype)
    elif epilogue == "residual":
        h_ref = refs[2]
        o_ref[...] = h_ref[...] + acc
    else:
        raise ValueError(epilogue)


def _mm_rows(m, epilogue):
    for tm in ((1072, 1056) if epilogue == "residual" else (1408, 1072, 1056)):
        if m % tm == 0:
            return tm
    return m


def _matmul(x, w, layer, col0, ncols, kblock=0, out_dtype=f32, residual=None):
    m = x.shape[0]
    tk = min(MM_K, x.shape[1])
    epilogue = "plain" if residual is None else "residual"
    tm = _mm_rows(m, epilogue)
    tn = min(MM_TN, ncols)
    off = col0 // tn
    assert col0 % tn == 0 and ncols % tn == 0 and x.shape[1] % tk == 0
    in_specs = [pl.BlockSpec((tm, tk), lambda j, i: (i, kblock)),
                pl.BlockSpec((None, tk, tn), lambda j, i: (layer, kblock, j + off))]
    args = [x, w]
    if residual is not None:
        in_specs.append(pl.BlockSpec((tm, tn), lambda j, i: (i, j)))
        args.append(residual)
    return pl.pallas_call(
        functools.partial(_mm_kernel, epilogue=epilogue, m_axis=1),
        grid=(ncols // tn, m // tm),
        in_specs=in_specs,
        out_specs=pl.BlockSpec((tm, tn), lambda j, i: (i, j)),
        out_shape=jax.ShapeDtypeStruct((m, ncols), out_dtype),
        scratch_shapes=[pltpu.VMEM((tk, tn), bf16)],
        compiler_params=_params(("parallel", "arbitrary")),
        name="matmul",
    )(*args)


def _out_project(h, act, w_out, layer):
    tk = min(MM_K, act.shape[1])
    for kb in range(act.shape[1] // tk):
        h = _matmul(act, w_out, layer, 0, w_out.shape[2], kblock=kb, residual=h)
    return h


def _grouped_matmul_gate(d, w_grp, layer, scale, z):
    m, e = d.shape
    _, ng, gk, gn = w_grp.shape
    tm = _mm_rows(m, "gate")
    tn = min(2 * MM_TN, gn)
    nb = gn // tn
    return pl.pallas_call(
        functools.partial(_mm_kernel, epilogue="gate", m_axis=2),
        grid=(ng, nb, m // tm),
        in_specs=[pl.BlockSpec((tm, gk), lambda g, j, i: (i, g)),
                  pl.BlockSpec((None, None, gk, tn), lambda g, j, i: (layer, g, 0, j)),
                  pl.BlockSpec((1, tn), lambda g, j, i: (0, g * nb + j)),
                  pl.BlockSpec((tm, tn), lambda g, j, i: (i, g * nb + j))],
        out_specs=pl.BlockSpec((tm, tn), lambda g, j, i: (i, g * nb + j)),
        out_shape=jax.ShapeDtypeStruct((m, e), bf16),
        scratch_shapes=[pltpu.VMEM((gk, tn), bf16)],
        compiler_params=_params(("parallel", "parallel", "arbitrary")),
        name="grouped_matmul_gate",
    )(d, w_grp, scale.reshape(1, e), z)


POOL_HALO = 16
POOL_ROWS = 16
POOL_LANES = 512


def _row_pieces(x):
    return [x[r:r + SUBLANES] for r in range(0, x.shape[0], SUBLANES)]


def _shift_rows(pieces, d):
    if d == SUBLANES:
        return [pieces[0]] + pieces[:-1]
    rolled = [pltpu.roll(p, d, 0) for p in pieces]
    keep = lax.broadcasted_iota(jnp.int32, pieces[0].shape, 0) >= d
    return [rolled[0]] + [jnp.where(keep, rolled[k], rolled[k - 1]) for k in range(1, len(pieces))]


def _window_sums(pieces, w):
    d = 1
    while d < w:
        pieces = [a + b for a, b in zip(pieces, _shift_rows(pieces, d))]
        d *= 2
    return pieces


def _pool_diff_prompt_kernel(halo_ref, cur_ref, o_ref, *, tm, gsz):
    t = pl.program_id(1)
    gi = (pl.program_id(2) * POOL_LANES) // gsz
    for widx, w in enumerate(POOL_WINDOWS):
        ctx = -(-(w - 1) // SUBLANES) * SUBLANES

        @pl.when(gi == widx)
        def _(w=w, ctx=ctx):
            for r in range(0, tm, POOL_ROWS):
                cur = cur_ref[r:r + POOL_ROWS, :]
                if r == 0:
                    prev = jnp.where(t > 0, halo_ref[POOL_HALO - ctx:POOL_HALO, :], 0.0)
                else:
                    prev = cur_ref[r - ctx:r, :]
                s = _window_sums(_row_pieces(prev) + _row_pieces(cur), w)[ctx // SUBLANES:]
                s = jnp.concatenate(s, axis=0)
                pos = lax.broadcasted_iota(jnp.int32, (POOL_ROWS, 1), 0) + (t * tm + r - FRONT)
                cnt = jnp.clip(pos + 1, 1, w).astype(f32)
                o_ref[r:r + POOL_ROWS, :] = (s / cnt - cur).astype(o_ref.dtype)


def _pool_diff_prompt(u, b, t):
    m, e = u.shape
    tm = 1056
    assert t % tm == 0 and tm % POOL_HALO == 0 and e % (len(POOL_WINDOWS) * POOL_LANES) == 0
    hb = tm // POOL_HALO
    nt = t // tm
    return pl.pallas_call(
        functools.partial(_pool_diff_prompt_kernel, tm=tm, gsz=e // len(POOL_WINDOWS)),
        grid=(b, nt, e // POOL_LANES),
        in_specs=[pl.BlockSpec((POOL_HALO, POOL_LANES),
                               lambda i, j, c: (jnp.maximum((i * nt + j) * hb - 1, 0), c)),
                  pl.BlockSpec((tm, POOL_LANES), lambda i, j, c: (i * nt + j, c))],
        out_specs=pl.BlockSpec((tm, POOL_LANES), lambda i, j, c: (i * nt + j, c)),
        out_shape=jax.ShapeDtypeStruct((m, e), bf16),
        compiler_params=_params(("parallel", "parallel", "parallel")),
        name="pool_diff_prompt",
    )(u, u)


def _pool_diff_sample_kernel(*refs, gsz, lanes):
    st_ref, u_ref = refs[0], refs[1]
    o_ref, ns_ref = refs[-2], refs[-1]
    gi = (pl.program_id(1) * lanes) // gsz
    for widx, w in enumerate(POOL_WINDOWS):
        @pl.when(gi == widx)
        def _(w=w):
            cur = u_ref[...]
            s = cur
            for i in range(1, w):
                s = s + st_ref[:, POOL_CTX - i, :]
            cnt = float(min(PAST_LEN + 1, w))
            o_ref[...] = (s / cnt - cur).astype(o_ref.dtype)
    for r in range(POOL_CTX - 1):
        ns_ref[:, r, :] = st_ref[:, r + 1, :]
    ns_ref[:, POOL_CTX - 1, :] = u_ref[...]


def _pool_diff_sample(states, layer, u, d, row0, new_states):
    _, b, _, e = states.shape
    gsz = e // len(POOL_WINDOWS)
    bb, lanes = 16, min(1024, gsz)
    assert b % bb == 0 and gsz % lanes == 0 and row0 % bb == 0
    st_spec = pl.BlockSpec((None, bb, POOL_CTX, lanes), lambda i, c: (layer, i, 0, c))
    row_spec = pl.BlockSpec((bb, lanes), lambda i, c: (row0 // bb + i, c))
    ns_shape, extra, extra_specs, alias = _stacked_out(new_states, layer, states.shape, states.dtype)
    return pl.pallas_call(
        functools.partial(_pool_diff_sample_kernel, gsz=gsz, lanes=lanes),
        grid=(b // bb, e // lanes),
        in_specs=[st_spec, row_spec, pl.BlockSpec(memory_space=pl.ANY)] + extra_specs,
        out_specs=[row_spec, st_spec],
        out_shape=[jax.ShapeDtypeStruct(d.shape, d.dtype), ns_shape],
        input_output_aliases={2: 0, **alias(3, 1)},
        compiler_params=_params(("parallel", "parallel")),
        name="pool_diff_sample",
    )(states, u, d, *extra)


CONV_HALO = 8
CONV_ROWS = 16
CONV_LANES = 1024


def _conv_act(acc, scale):
    y = _silu(acc)
    if scale is None:
        return y
    parts = []
    for h in range(y.shape[1] // HEAD_DIM):
        yh = y[:, h * HEAD_DIM:(h + 1) * HEAD_DIM]
        parts.append(yh * lax.rsqrt(jnp.sum(yh * yh, axis=-1, keepdims=True) + EPS) * scale)
    return jnp.concatenate(parts, axis=1)


def _conv_branches(c, key_dim, body):
    normalise = c * CONV_LANES < 2 * key_dim
    q_scale = jnp.where(c * CONV_LANES < key_dim, HEAD_DIM ** -0.5, 1.0)
    pl.when(normalise)(lambda: body(q_scale))
    pl.when(jnp.logical_not(normalise))(lambda: body(None))


def _conv_prompt_kernel(halo_ref, cur_ref, w_ref, o_ref, *, tm, key_dim):
    t = pl.program_id(1)

    def body(scale):
        for r in range(0, tm, CONV_ROWS):
            if r == 0:
                prev = jnp.where(t > 0, halo_ref[...], 0.0)
            else:
                prev = cur_ref[r - CONV_HALO:r, :]
            cur = cur_ref[r:r + CONV_ROWS, :]
            pieces = _row_pieces(prev) + _row_pieces(cur)
            acc = cur * w_ref[CONV_WIDTH - 1:CONV_WIDTH, :]
            for d in range(1, CONV_WIDTH):
                tap = CONV_WIDTH - 1 - d
                back = jnp.concatenate(_shift_rows(pieces, d)[CONV_HALO // SUBLANES:], axis=0)
                acc = acc + back * w_ref[tap:tap + 1, :]
            o_ref[r:r + CONV_ROWS, :] = _conv_act(acc, scale)

    _conv_branches(pl.program_id(2), key_dim, body)


def _conv_prompt(x, conv_w, key_dim, b, t):
    m, c = x.shape
    tm = 1056
    assert t % tm == 0 and tm % CONV_ROWS == 0 and tm % CONV_HALO == 0 and key_dim % CONV_LANES == 0
    hb = tm // CONV_HALO
    nt = t // tm
    return pl.pallas_call(
        functools.partial(_conv_prompt_kernel, tm=tm, key_dim=key_dim),
        grid=(b, nt, c // CONV_LANES),
        in_specs=[pl.BlockSpec((CONV_HALO, CONV_LANES),
                               lambda i, j, l: (jnp.maximum((i * nt + j) * hb - 1, 0), l)),
                  pl.BlockSpec((tm, CONV_LANES), lambda i, j, l: (i * nt + j, l)),
                  pl.BlockSpec((CONV_WIDTH, CONV_LANES), lambda i, j, l: (0, l))],
        out_specs=pl.BlockSpec((tm, CONV_LANES), lambda i, j, l: (i * nt + j, l)),
        out_shape=jax.ShapeDtypeStruct((m, c), f32),
        compiler_params=_params(("parallel", "parallel", "parallel")),
        name="conv_prompt",
    )(x, x, conv_w)


def _conv_sample_kernel(*refs, key_dim):
    st_ref, x_ref, w_ref = refs[0], refs[1], refs[2]
    o_ref, ns_ref = refs[-2], refs[-1]

    def body(scale):
        acc = x_ref[...] * w_ref[CONV_WIDTH - 1:CONV_WIDTH, :]
        for i in range(CONV_WIDTH - 1):
            acc = acc + st_ref[:, i, :] * w_ref[i:i + 1, :]
        o_ref[...] = _conv_act(acc, scale)

    _conv_branches(pl.program_id(0), key_dim, body)
    for r in range(CONV_WIDTH - 2):
        ns_ref[:, r, :] = st_ref[:, r + 1, :]
    ns_ref[:, CONV_WIDTH - 2, :] = x_ref[...]


def _conv_sample(states, layer, x, act, row0, conv_w, key_dim, new_states):
    b = states.shape[1]
    c = x.shape[1]
    assert row0 % b == 0
    st_spec = pl.BlockSpec((None, b, CONV_WIDTH - 1, CONV_LANES), lambda l: (layer, 0, 0, l))
    row_spec = pl.BlockSpec((b, CONV_LANES), lambda l: (row0 // b, l))
    ns_shape, extra, extra_specs, alias = _stacked_out(new_states, layer, states.shape, states.dtype)
    return pl.pallas_call(
        functools.partial(_conv_sample_kernel, key_dim=key_dim),
        grid=(c // CONV_LANES,),
        in_specs=[st_spec, row_spec,
                  pl.BlockSpec((CONV_WIDTH, CONV_LANES), lambda l: (0, l)),
                  pl.BlockSpec(memory_space=pl.ANY)] + extra_specs,
        out_specs=[row_spec, st_spec],
        out_shape=[jax.ShapeDtypeStruct(act.shape, act.dtype), ns_shape],
        input_output_aliases={3: 0, **alias(4, 1)},
        compiler_params=_params(("parallel",)),
        name="conv_sample",
    )(states, x, conv_w, act, *extra)


def _gate_values(ba_ref, a_ref, dt_ref):
    x = ba_ref[...]
    xa = x + dt_ref[...]
    softplus = jnp.maximum(xa, 0.0) + jnp.log1p(jnp.exp(-jnp.abs(xa)))
    return jax.nn.sigmoid(x), -jnp.exp(a_ref[...]) * softplus


def _gate_chunk_kernel(ba_ref, a_ref, dt_ref, o_ref, *, n_heads):
    beta, g = _gate_values(ba_ref, a_ref, dt_ref)
    row = lax.broadcasted_iota(jnp.int32, (CHUNK, 1), 0)
    g = jnp.where(row < jnp.where(pl.program_id(1) == 0, FRONT, 0), 0.0, g)
    shift = 1
    while shift < CHUNK:
        g = g + jnp.where(row >= shift, pltpu.roll(g, shift, 0), 0.0)
        shift *= 2
    lane = lax.broadcasted_iota(jnp.int32, g.shape, 1)
    o_ref[...] = jnp.where(lane < n_heads, beta, g)


def _gates_chunked(ba, a_pad, dt_pad, b, t):
    w = ba.shape[1]
    nc = t // CHUNK
    return pl.pallas_call(
        functools.partial(_gate_chunk_kernel, n_heads=w // 2),
        grid=(b, nc),
        in_specs=[pl.BlockSpec((CHUNK, w), lambda i, j: (i * nc + j, 0)),
                  pl.BlockSpec((1, w), lambda i, j: (0, 0)),
                  pl.BlockSpec((1, w), lambda i, j: (0, 0))],
        out_specs=pl.BlockSpec((CHUNK, w), lambda i, j: (i * nc + j, 0)),
        out_shape=jax.ShapeDtypeStruct((b * t, w), f32),
        compiler_params=_params(("parallel", "parallel")),
        name="gdn_gates_chunked",
    )(ba, a_pad, dt_pad)


def _gate_step_kernel(ba_ref, a_ref, dt_ref, beta_ref, g_ref, *, n_heads):
    beta, g = _gate_values(ba_ref, a_ref, dt_ref)
    rows = beta.shape[0]
    for h in range(n_heads):
        sl = slice(h * HEAD_DIM, (h + 1) * HEAD_DIM)
        beta_ref[:, sl] = jnp.broadcast_to(beta[:, h:h + 1], (rows, HEAD_DIM))
        g_ref[:, sl] = jnp.broadcast_to(g[:, n_heads + h:n_heads + h + 1], (rows, HEAD_DIM))


def _gates_step(ba, a_pad, dt_pad, row0, b):
    w = ba.shape[1]
    n_heads = w // 2
    wide = n_heads * HEAD_DIM
    assert row0 % b == 0
    return pl.pallas_call(
        functools.partial(_gate_step_kernel, n_heads=n_heads),
        grid=(1,),
        in_specs=[pl.BlockSpec((b, w), lambda i: (row0 // b, 0)),
                  pl.BlockSpec((1, w), lambda i: (0, 0)),
                  pl.BlockSpec((1, w), lambda i: (0, 0))],
        out_specs=[pl.BlockSpec((b, wide), lambda i: (0, 0)),
                   pl.BlockSpec((b, wide), lambda i: (0, 0))],
        out_shape=[jax.ShapeDtypeStruct((b, wide), f32), jax.ShapeDtypeStruct((b, wide), f32)],
        compiler_params=_params(("arbitrary",)),
        name="gdn_gates_step",
    )(ba, a_pad, dt_pad)


def _gate_params(a_log, dt_bias):
    zeros = jnp.zeros(a_log.shape, f32)
    w = 2 * a_log.shape[0]
    return (jnp.concatenate([zeros, a_log.astype(f32)]).reshape(1, w),
            jnp.concatenate([zeros, dt_bias.astype(f32)]).reshape(1, w))


def _dot_nt(a, b):
    return lax.dot_general(a, b, (((1,), (1,)), ((), ())), preferred_element_type=f32)


def _dot_tn(a, b):
    return lax.dot_general(a, b, (((0,), (0,)), ((), ())), preferred_element_type=f32)


def _dot16(a, b):
    return jnp.dot(a.astype(bf16), b.astype(bf16), preferred_element_type=f32)


def _stack(x, idx):
    return jnp.concatenate([x[:, i * HEAD_DIM:(i + 1) * HEAD_DIM] for i in idx], axis=0)


def _gdn_chunk_kernel(q_ref, k_ref, v_ref, bg_ref, gcr_ref, z_ref, ng_ref,
                      o_ref, sout_ref, s_ref):
    c = pl.program_id(2)

    @pl.when(c == 0)
    def _():
        s_ref[...] = jnp.zeros_like(s_ref)

    quads = range(QUADS_PER_STEP)
    heads = tuple(range(QUAD))
    khead = tuple(h // 2 for h in heads)
    rows = QUAD * CHUNK
    kq_w = QUAD // 2 * HEAD_DIM
    v_w = QUAD * HEAD_DIM

    def head_rows(h):
        return slice(h * CHUNK, (h + 1) * CHUNK)

    ri = lax.broadcasted_iota(jnp.int32, (rows, rows), 0)
    ci = lax.broadcasted_iota(jnp.int32, (rows, rows), 1)
    shift = CHUNK.bit_length() - 1
    same = lax.shift_right_logical(ri, shift) == lax.shift_right_logical(ci, shift)
    lower = same & (ri >= ci)
    strict = ri > ci

    bg = bg_ref[...]
    n_step = QUAD * QUADS_PER_STEP
    k16, kb16, q16, rhs, qd, kd16, decay, g_last = [], [], [], [], [], [], [], []
    for i in quads:
        kq = slice(i * kq_w, (i + 1) * kq_w)
        vs = slice(i * v_w, (i + 1) * v_w)
        kst = _stack(k_ref[:, kq], khead)
        qst = _stack(q_ref[:, kq], khead)
        cols = [i * QUAD + h for h in heads]
        bcol = [jnp.broadcast_to(bg[:, n:n + 1], (CHUNK, HEAD_DIM)) for n in cols]
        gcol = [jnp.broadcast_to(bg[:, n_step + n:n_step + n + 1], (CHUNK, HEAD_DIM)) for n in cols]
        bst = jnp.concatenate(bcol, axis=0)
        gst = jnp.concatenate(gcol, axis=0)
        last = [g[CHUNK - 1:CHUNK] for g in gcol]
        glast = jnp.concatenate([jnp.broadcast_to(r, (CHUNK, HEAD_DIM)) for r in last], axis=0)
        eg = jnp.exp(gst)
        kb = kst * bst
        k16.append(kst.astype(bf16))
        kb16.append(kb.astype(bf16))
        q16.append(qst.astype(bf16))
        rhs.append(jnp.concatenate([_stack(v_ref[:, vs], heads) * bst, kb * eg], axis=1))
        qd.append(qst * eg)
        kd16.append((kst * jnp.exp(glast - gst)).astype(bf16))
        diff = jnp.concatenate([gst] * (rows // HEAD_DIM), axis=1) - gcr_ref[i]
        decay.append(jnp.exp(jnp.where(lower, diff, -jnp.inf)))
        g_last.append([jnp.exp(r) for r in last])

    pw = [-jnp.where(strict, _dot_nt(kb16[i], k16[i]) * decay[i], 0.0) for i in quads]
    attn16 = [(_dot_nt(q16[i], k16[i]) * decay[i]).astype(bf16) for i in quads]

    sol = [rhs[i] + _dot16(pw[i], rhs[i]) for i in quads]
    for _ in range(NEUMANN_LEVELS - 1):
        pw16 = [p.astype(bf16) for p in pw]
        pw = [jnp.dot(p, p, preferred_element_type=f32) for p in pw16]
        sol = [sol[i] + _dot16(pw[i], sol[i]) for i in quads]

    x = [[_dot16(jnp.concatenate([sol[i][head_rows(h), HEAD_DIM:], qd[i][head_rows(h)]], axis=0),
                 s_ref[i * QUAD + h]) for h in heads] for i in quads]
    vn16 = [jnp.concatenate([sol[i][head_rows(h), :HEAD_DIM] - x[i][h][:CHUNK] for h in heads],
                            axis=0).astype(bf16) for i in quads]
    o = [jnp.concatenate([x[i][h][CHUNK:] for h in heads], axis=0)
         + jnp.dot(attn16[i], vn16[i], preferred_element_type=f32) for i in quads]
    for i in quads:
        for h in heads:
            s_ref[i * QUAD + h] = (s_ref[i * QUAD + h] * g_last[i][h]
                                   + _dot_tn(kd16[i][head_rows(h)], vn16[i][head_rows(h)]))

    for i in quads:
        on = o[i] * lax.rsqrt(jnp.mean(o[i] * o[i], axis=-1, keepdims=True) + EPS) * ng_ref[...]
        for h in heads:
            sl = slice(i * v_w + h * HEAD_DIM, i * v_w + (h + 1) * HEAD_DIM)
            o_ref[:, sl] = (on[head_rows(h)] * _silu(z_ref[:, sl])).astype(o_ref.dtype)

    @pl.when(c == pl.num_programs(2) - 1)
    def _():
        sout_ref[...] = s_ref[...]


def _gdn_chunked(qkv, bg, gc_rows, z, norm_g, batch):
    m, value_dim = z.shape
    key_dim = (qkv.shape[1] - value_dim) // 2
    n_heads = value_dim // HEAD_DIM
    n_chunks = bg.shape[1] // batch // CHUNK
    heads_per_step = QUAD * QUADS_PER_STEP
    n_groups = n_heads // heads_per_step
    kq_w = heads_per_step // 2 * HEAD_DIM
    v_w = heads_per_step * HEAD_DIM

    def row_spec(width, col0=0):
        off = col0 // width
        return pl.BlockSpec((CHUNK, width), lambda b, g, c: (b * n_chunks + c, g + off))

    return pl.pallas_call(
        _gdn_chunk_kernel,
        grid=(batch, n_groups, n_chunks),
        in_specs=[row_spec(kq_w), row_spec(kq_w, key_dim), row_spec(v_w, 2 * key_dim),
                  pl.BlockSpec((None, CHUNK, 2 * heads_per_step),
                               lambda b, g, c: (g, b * n_chunks + c, 0)),
                  pl.BlockSpec((None, None, QUADS_PER_STEP, 1, QUAD * CHUNK),
                               lambda b, g, c: (b, c, g, 0, 0)),
                  row_spec(v_w),
                  pl.BlockSpec((1, HEAD_DIM), lambda b, g, c: (0, 0))],
        out_specs=[row_spec(v_w),
                   pl.BlockSpec((None, heads_per_step, HEAD_DIM, HEAD_DIM),
                                lambda b, g, c: (b, g, 0, 0))],
        out_shape=[jax.ShapeDtypeStruct((m, value_dim), bf16),
                   jax.ShapeDtypeStruct((batch, n_heads, HEAD_DIM, HEAD_DIM), f32)],
        scratch_shapes=[pltpu.VMEM((heads_per_step, HEAD_DIM, HEAD_DIM), f32)],
        compiler_params=_params(("parallel", "parallel", "arbitrary")),
        name="gdn_chunked",
    )(qkv, qkv, qkv, bg, gc_rows, z, norm_g.reshape(1, HEAD_DIM))


def _gdn_step_kernel(*refs, n_heads):
    qt_ref, kt_ref, v_ref, beta_ref, g_ref, z_ref, ng_ref, s_ref = refs[:8]
    o_ref, sout_ref, kcol_ref, qcol_ref, eg_ref, oraw_ref = refs[-6:]
    n_k = qt_ref.shape[1]
    rep = n_heads // n_k
    for kh in range(n_k):
        kcol_ref[kh] = jnp.broadcast_to(kt_ref[:, kh:kh + 1], (HEAD_DIM, HEAD_DIM))
        qcol_ref[kh] = jnp.broadcast_to(qt_ref[:, kh:kh + 1], (HEAD_DIM, HEAD_DIM))
    eg_ref[...] = jnp.exp(g_ref[...])
    for h in range(n_heads):
        kcol = kcol_ref[h // rep]
        s = s_ref[h] * eg_ref[h:h + 1, :]
        kv = jnp.sum(kcol * s, axis=0, keepdims=True)
        delta = (v_ref[h:h + 1, :] - kv) * beta_ref[h:h + 1, :]
        s = s + kcol * delta
        sout_ref[h] = s
        oraw_ref[h:h + 1, :] = jnp.sum(qcol_ref[h // rep] * s, axis=0, keepdims=True)
    o = oraw_ref[...]
    on = o * lax.rsqrt(jnp.mean(o * o, axis=-1, keepdims=True) + EPS) * ng_ref[...]
    o_ref[...] = on * _silu(z_ref[...])


def _gdn_step(qt, kt, v3, beta3, g3, z3, norm_g, states, layer, new_states):
    b, n_heads, _ = v3.shape
    n_k = qt.shape[2]

    def head_spec():
        return pl.BlockSpec((None, n_heads, HEAD_DIM), lambda i: (i, 0, 0))

    def t_spec():
        return pl.BlockSpec((None, HEAD_DIM, n_k), lambda i: (i, 0, 0))

    s_spec = pl.BlockSpec((None, None, n_heads, HEAD_DIM, HEAD_DIM),
                          lambda i: (layer, i, 0, 0, 0))
    ns_shape, extra, extra_specs, alias = _stacked_out(new_states, layer, states.shape, f32)
    return pl.pallas_call(
        functools.partial(_gdn_step_kernel, n_heads=n_heads),
        grid=(b,),
        in_specs=[t_spec(), t_spec(), head_spec(), head_spec(), head_spec(), head_spec(),
                  pl.BlockSpec((1, HEAD_DIM), lambda i: (0, 0)), s_spec] + extra_specs,
        out_specs=[head_spec(), s_spec],
        out_shape=[jax.ShapeDtypeStruct((b, n_heads, HEAD_DIM), f32), ns_shape],
        scratch_shapes=[pltpu.VMEM((n_k, HEAD_DIM, HEAD_DIM), f32),
                        pltpu.VMEM((n_k, HEAD_DIM, HEAD_DIM), f32),
                        pltpu.VMEM((n_heads, HEAD_DIM), f32),
                        pltpu.VMEM((n_heads, HEAD_DIM), f32)],
        input_output_aliases=alias(8, 1),
        compiler_params=_params(("parallel",)),
        name="gdn_step",
    )(qt, kt, v3, beta3, g3, z3, norm_g.reshape(1, HEAD_DIM), states, *extra)


def _sequence_tails(x, batch, t, n):
    return jnp.stack([lax.slice_in_dim(x, (b + 1) * t - n, (b + 1) * t, axis=0)
                      for b in range(batch)])


def _pool_layer(h, rows_p, states, layer, new_states, w_in, w_grp, scale, w_out, norm_g, batch):
    e = w_grp.shape[1] * w_grp.shape[2]
    t = rows_p // batch
    xn = _rmsnorm(h, norm_g, bf16)
    u = _matmul(xn, w_in, layer, 0, e)
    z = _matmul(xn, w_in, layer, e, e)
    d = _pool_diff_prompt(u, batch, t)
    d, new_states = _pool_diff_sample(states, layer, u, d, rows_p, new_states)
    h = _out_project(h, _grouped_matmul_gate(d, w_grp, layer, scale, z), w_out, layer)
    pool_p = _sequence_tails(u, batch, t, POOL_CTX)
    return h, pool_p, new_states


def _gdn_layer(h, rows_p, conv_states, ssm_states, layer, new_conv, new_ssm, w_in, conv_w, a_log,
               dt_bias, head_g, w_out, norm_g, batch):
    value_dim = w_out.shape[1]
    n_heads = value_dim // HEAD_DIM
    conv_dim = conv_w.shape[1]
    key_dim = (conv_dim - value_dim) // 2
    n_k = key_dim // HEAD_DIM
    t = rows_p // batch
    bs = h.shape[0] - rows_p

    xn = _rmsnorm(h, norm_g, bf16)
    qkv = _matmul(xn, w_in, layer, 0, conv_dim)
    z = _matmul(xn, w_in, layer, conv_dim, value_dim)
    ba = _matmul(xn, w_in, layer, conv_dim + value_dim, 2 * n_heads)
    act = _conv_prompt(qkv, conv_w, key_dim, batch, t)
    act, new_conv = _conv_sample(conv_states, layer, qkv, act, rows_p, conv_w, key_dim, new_conv)
    a_pad, dt_pad = _gate_params(a_log, dt_bias)

    bg = _gates_chunked(ba, a_pad, dt_pad, batch, t)
    gc = bg[:, n_heads:]
    gc_rows = gc.reshape(batch, t // CHUNK, CHUNK, n_heads // QUAD, QUAD)
    gc_rows = jnp.transpose(gc_rows, (0, 1, 3, 4, 2)).reshape(
        batch, t // CHUNK, n_heads // QUAD, 1, QUAD * CHUNK)
    n_step = QUAD * QUADS_PER_STEP
    bg = jnp.concatenate([bg[:, :n_heads].reshape(-1, n_heads // n_step, n_step),
                          gc.reshape(-1, n_heads // n_step, n_step)], axis=2)
    o, ssm_p = _gdn_chunked(act, jnp.transpose(bg, (1, 0, 2)), gc_rows, z, head_g, batch)
    conv_p = _sequence_tails(qkv, batch, t, CONV_WIDTH - 1)

    beta_b, g_b = _gates_step(ba, a_pad, dt_pad, rows_p, bs)
    act_s = act[rows_p:]
    qt = jnp.transpose(act_s[:, :key_dim].reshape(bs, n_k, HEAD_DIM), (0, 2, 1))
    kt = jnp.transpose(act_s[:, key_dim:2 * key_dim].reshape(bs, n_k, HEAD_DIM), (0, 2, 1))
    per_head = (bs, n_heads, HEAD_DIM)
    o_s, new_ssm = _gdn_step(qt, kt, act_s[:, 2 * key_dim:].reshape(per_head),
                             beta_b.reshape(per_head), g_b.reshape(per_head),
                             z[rows_p:].reshape(per_head), head_g, ssm_states, layer, new_ssm)
    o = lax.dynamic_update_slice(o, o_s.reshape(bs, value_dim).astype(bf16), (rows_p, 0))
    h = _out_project(h, o, w_out, layer)
    return h, conv_p, ssm_p, new_conv, new_ssm


def kernel(x_prompt, x_sample, state_pool, state_conv, state_ssm, meta_tokens, norm_g, final_norm_g, pool_w_in, pool_w_grp, pool_scale, pool_w_out, gdn_w_in, gdn_conv_w, gdn_A_log, gdn_dt_bias, gdn_norm_g, gdn_w_out):
    dt = x_prompt.dtype
    batch, seq, d = x_prompt.shape
    depth = norm_g.shape[0]
    meta = jnp.broadcast_to(meta_tokens.astype(dt)[None], (batch, N_META, d))
    hp = jnp.concatenate([jnp.zeros((batch, FRONT, d), dt), meta, x_prompt], axis=1)
    t = hp.shape[1]
    rows_p = batch * t
    h = jnp.concatenate([hp.reshape(rows_p, d), x_sample.reshape(x_sample.shape[0], d)], axis=0)
    pool_p, conv_p, ssm_p = [], [], []
    pool_s = conv_s = ssm_s = None
    for i in range(depth):
        j = i // 2
        if i % 2 == 0:
            h, pp, pool_s = _pool_layer(h, rows_p, state_pool, j, pool_s, pool_w_in, pool_w_grp,
                                        pool_scale[j], pool_w_out, norm_g[i], batch)
            pool_p.append(pp)
        else:
            h, cp, sp, conv_s, ssm_s = _gdn_layer(
                h, rows_p, state_conv, state_ssm, j, conv_s, ssm_s, gdn_w_in, gdn_conv_w[j],
                gdn_A_log[j], gdn_dt_bias[j], gdn_norm_g[j], gdn_w_out, norm_g[i], batch)
            conv_p.append(cp)
            ssm_p.append(sp)
    y_prompt = _final_norm_prompt(h, final_norm_g, batch, t, seq)
    y_sample = _rmsnorm(h[rows_p:], final_norm_g, dt).reshape(x_sample.shape)
    return (y_prompt, y_sample, jnp.stack(pool_p), jnp.stack(conv_p), jnp.stack(ssm_p),
            pool_s, conv_s, ssm_s)
```

```python
import functools

import jax
import jax.numpy as jnp
from jax import lax
from jax.experimental import pallas as pl
from jax.experimental.pallas import tpu as pltpu

N_META = 16
PAST_LEN = 16384
POOL_WINDOWS = (2, 4, 8, 16)
POOL_CTX = max(POOL_WINDOWS) - 1
HEAD_DIM = 128
CONV_WIDTH = 4
CHUNK = 64
EPS = 1e-6
FRONT = (-N_META) % CHUNK
QUAD = 4
QUADS_PER_STEP = 16
GATE_CHUNKS = 3
NEUMANN_LEVELS = 6

VMEM_LIMIT = 56 * 1024 * 1024
SUBLANES = 8

f32 = jnp.float32
bf16 = jnp.bfloat16


def _params(sem):
    return pltpu.CompilerParams(dimension_semantics=sem, vmem_limit_bytes=VMEM_LIMIT)


def _silu(x):
    return x * jax.nn.sigmoid(x)


def _stacked_out(stack, layer, shape, dtype):
    if stack is None:
        return jax.ShapeDtypeStruct(shape, dtype), [], [], lambda n_in, n_out: {}
    return (jax.ShapeDtypeStruct(stack.shape, stack.dtype), [stack],
            [pl.BlockSpec(memory_space=pl.ANY)], lambda n_in, n_out: {n_in: n_out})


def _rmsnorm_kernel(x_ref, g_ref, o_ref):
    x = x_ref[...]
    y = x * lax.rsqrt(jnp.mean(x * x, axis=-1, keepdims=True) + EPS)
    o_ref[...] = (y * g_ref[...]).astype(o_ref.dtype)


def _rmsnorm(x, g, out_dtype):
    m, d = x.shape
    tm = next((c for c in (256, 536) if m % c == 0), m)
    return pl.pallas_call(
        _rmsnorm_kernel,
        grid=(m // tm,),
        in_specs=[pl.BlockSpec((tm, d), lambda i: (i, 0)),
                  pl.BlockSpec((1, d), lambda i: (0, 0))],
        out_specs=pl.BlockSpec((tm, d), lambda i: (i, 0)),
        out_shape=jax.ShapeDtypeStruct((m, d), out_dtype),
        compiler_params=_params(("parallel",)),
        name="rmsnorm",
    )(x, g.reshape(1, d))


def _final_norm_prompt(h, g, b, t, seq):
    d = h.shape[1]
    skip = (t - seq) // CHUNK
    return pl.pallas_call(
        _rmsnorm_kernel,
        grid=(b, seq // CHUNK),
        in_specs=[pl.BlockSpec((CHUNK, d), lambda i, j: (i * (t // CHUNK) + j + skip, 0)),
                  pl.BlockSpec((1, d), lambda i, j: (0, 0))],
        out_specs=pl.BlockSpec((None, CHUNK, d), lambda i, j: (i, j, 0)),
        out_shape=jax.ShapeDtypeStruct((b, seq, d), h.dtype),
        compiler_params=_params(("parallel", "parallel")),
        name="final_norm",
    )(h, g.reshape(1, d))


MM_K = 4096
MM_TN = 512


def _mm_kernel(*refs, epilogue, m_axis):
    x_ref, w_ref = refs[0], refs[1]
    o_ref, w16_ref = refs[-2], refs[-1]

    @pl.when(pl.program_id(m_axis) == 0)
    def _():
        w16_ref[...] = w_ref[...].astype(bf16)

    acc = jnp.dot(x_ref[...], w16_ref[...], preferred_element_type=f32)
    if epilogue == "plain":
        o_ref[...] = acc.astype(o_ref.dtype)
    elif epilogue == "gate":
        scale_ref, z_ref = refs[2], refs[3]
        o_ref[...] = (acc * scale_ref[...] * _silu(z_ref[...])).astype(o_ref.dtype)
    elif epilogue == "residual":
        h_ref = refs[2]
        o_ref[...] = h_ref[...] + acc
    else:
        raise ValueError(epilogue)


def _mm_rows(m, epilogue):
    for tm in ((1072, 1056) if epilogue == "residual" else (1408, 1072, 1056)):
        if m % tm == 0:
            return tm
    return m


def _matmul(x, w, layer, col0, ncols, kblock=0, out_dtype=f32, residual=None):
    m = x.shape[0]
    tk = min(MM_K, x.shape[1])
    epilogue = "plain" if residual is None else "residual"
    tm = _mm_rows(m, epilogue)
    tn = min(MM_TN, ncols)
    off = col0 // tn
    assert col0 % tn == 0 and ncols % tn == 0 and x.shape[1] % tk == 0
    in_specs = [pl.BlockSpec((tm, tk), lambda j, i: (i, kblock)),
                pl.BlockSpec((None, tk, tn), lambda j, i: (layer, kblock, j + off))]
    args = [x, w]
    if residual is not None:
        in_specs.append(pl.BlockSpec((tm, tn), lambda j, i: (i, j)))
        args.append(residual)
    return pl.pallas_call(
        functools.partial(_mm_kernel, epilogue=epilogue, m_axis=1),
        grid=(ncols // tn, m // tm),
        in_specs=in_specs,
        out_specs=pl.BlockSpec((tm, tn), lambda j, i: (i, j)),
        out_shape=jax.ShapeDtypeStruct((m, ncols), out_dtype),
        scratch_shapes=[pltpu.VMEM((tk, tn), bf16)],
        compiler_params=_params(("parallel", "arbitrary")),
        name="matmul",
    )(*args)


def _out_project(h, act, w_out, layer):
    tk = min(MM_K, act.shape[1])
    for kb in range(act.shape[1] // tk):
        h = _matmul(act, w_out, layer, 0, w_out.shape[2], kblock=kb, residual=h)
    return h


def _grouped_matmul_gate(d, w_grp, layer, scale, z):
    m, e = d.shape
    _, ng, gk, gn = w_grp.shape
    tm = _mm_rows(m, "gate")
    tn = min(2 * MM_TN, gn)
    nb = gn // tn
    return pl.pallas_call(
        functools.partial(_mm_kernel, epilogue="gate", m_axis=2),
        grid=(ng, nb, m // tm),
        in_specs=[pl.BlockSpec((tm, gk), lambda g, j, i: (i, g)),
                  pl.BlockSpec((None, None, gk, tn), lambda g, j, i: (layer, g, 0, j)),
                  pl.BlockSpec((1, tn), lambda g, j, i: (0, g * nb + j)),
                  pl.BlockSpec((tm, tn), lambda g, j, i: (i, g * nb + j))],
        out_specs=pl.BlockSpec((tm, tn), lambda g, j, i: (i, g * nb + j)),
        out_shape=jax.ShapeDtypeStruct((m, e), bf16),
        scratch_shapes=[pltpu.VMEM((gk, tn), bf16)],
        compiler_params=_params(("parallel", "parallel", "arbitrary")),
        name="grouped_matmul_gate",
    )(d, w_grp, scale.reshape(1, e), z)


POOL_HALO = 16
POOL_ROWS = 16
POOL_LANES = 512


def _row_pieces(x):
    return [x[r:r + SUBLANES] for r in range(0, x.shape[0], SUBLANES)]


def _shift_rows(pieces, d):
    if d == SUBLANES:
        return [pieces[0]] + pieces[:-1]
    rolled = [pltpu.roll(p, d, 0) for p in pieces]
    keep = lax.broadcasted_iota(jnp.int32, pieces[0].shape, 0) >= d
    return [rolled[0]] + [jnp.where(keep, rolled[k], rolled[k - 1]) for k in range(1, len(pieces))]


def _window_sums(pieces, w):
    d = 1
    while d < w:
        pieces = [a + b for a, b in zip(pieces, _shift_rows(pieces, d))]
        d *= 2
    return pieces


def _pool_diff_prompt_kernel(halo_ref, cur_ref, o_ref, *, tm, gsz):
    t = pl.program_id(1)
    gi = (pl.program_id(2) * POOL_LANES) // gsz
    for widx, w in enumerate(POOL_WINDOWS):
        ctx = -(-(w - 1) // SUBLANES) * SUBLANES

        @pl.when(gi == widx)
        def _(w=w, ctx=ctx):
            for r in range(0, tm, POOL_ROWS):
                cur = cur_ref[r:r + POOL_ROWS, :]
                if r == 0:
                    prev = jnp.where(t > 0, halo_ref[POOL_HALO - ctx:POOL_HALO, :], 0.0)
                else:
                    prev = cur_ref[r - ctx:r, :]
                s = _window_sums(_row_pieces(prev) + _row_pieces(cur), w)[ctx // SUBLANES:]
                s = jnp.concatenate(s, axis=0)
                pos = lax.broadcasted_iota(jnp.int32, (POOL_ROWS, 1), 0) + (t * tm + r - FRONT)
                cnt = jnp.clip(pos + 1, 1, w).astype(f32)
                o_ref[r:r + POOL_ROWS, :] = (s / cnt - cur).astype(o_ref.dtype)


def _pool_diff_prompt(u, b, t):
    m, e = u.shape
    tm = 1056
    assert t % tm == 0 and tm % POOL_HALO == 0 and e % (len(POOL_WINDOWS) * POOL_LANES) == 0
    hb = tm // POOL_HALO
    nt = t // tm
    return pl.pallas_call(
        functools.partial(_pool_diff_prompt_kernel, tm=tm, gsz=e // len(POOL_WINDOWS)),
        grid=(b, nt, e // POOL_LANES),
        in_specs=[pl.BlockSpec((POOL_HALO, POOL_LANES),
                               lambda i, j, c: (jnp.maximum((i * nt + j) * hb - 1, 0), c)),
                  pl.BlockSpec((tm, POOL_LANES), lambda i, j, c: (i * nt + j, c))],
        out_specs=pl.BlockSpec((tm, POOL_LANES), lambda i, j, c: (i * nt + j, c)),
        out_shape=jax.ShapeDtypeStruct((m, e), bf16),
        compiler_params=_params(("parallel", "parallel", "parallel")),
        name="pool_diff_prompt",
    )(u, u)


def _pool_diff_sample_kernel(*refs, gsz, lanes):
    st_ref, u_ref = refs[0], refs[1]
    o_ref, ns_ref = refs[-2], refs[-1]
    gi = (pl.program_id(1) * lanes) // gsz
    for widx, w in enumerate(POOL_WINDOWS):
        @pl.when(gi == widx)
        def _(w=w):
            cur = u_ref[...]
            s = cur
            for i in range(1, w):
                s = s + st_ref[:, POOL_CTX - i, :]
            cnt = float(min(PAST_LEN + 1, w))
            o_ref[...] = (s / cnt - cur).astype(o_ref.dtype)
    for r in range(POOL_CTX - 1):
        ns_ref[:, r, :] = st_ref[:, r + 1, :]
    ns_ref[:, POOL_CTX - 1, :] = u_ref[...]


def _pool_diff_sample(states, layer, u, d, row0, new_states):
    _, b, _, e = states.shape
    gsz = e // len(POOL_WINDOWS)
    bb, lanes = 16, min(1024, gsz)
    assert b % bb == 0 and gsz % lanes == 0 and row0 % bb == 0
    st_spec = pl.BlockSpec((None, bb, POOL_CTX, lanes), lambda i, c: (layer, i, 0, c))
    row_spec = pl.BlockSpec((bb, lanes), lambda i, c: (row0 // bb + i, c))
    ns_shape, extra, extra_specs, alias = _stacked_out(new_states, layer, states.shape, states.dtype)
    return pl.pallas_call(
        functools.partial(_pool_diff_sample_kernel, gsz=gsz, lanes=lanes),
        grid=(b // bb, e // lanes),
        in_specs=[st_spec, row_spec, pl.BlockSpec(memory_space=pl.ANY)] + extra_specs,
        out_specs=[row_spec, st_spec],
        out_shape=[jax.ShapeDtypeStruct(d.shape, d.dtype), ns_shape],
        input_output_aliases={2: 0, **alias(3, 1)},
        compiler_params=_params(("parallel", "parallel")),
        name="pool_diff_sample",
    )(states, u, d, *extra)


CONV_HALO = 8
CONV_ROWS = 16
CONV_LANES = 1024


def _conv_act(acc, scale):
    y = _silu(acc)
    if scale is None:
        return y
    parts = []
    for h in range(y.shape[1] // HEAD_DIM):
        yh = y[:, h * HEAD_DIM:(h + 1) * HEAD_DIM]
        parts.append(yh * lax.rsqrt(jnp.sum(yh * yh, axis=-1, keepdims=True) + EPS) * scale)
    return jnp.concatenate(parts, axis=1)


def _conv_branches(c, key_dim, body):
    normalise = c * CONV_LANES < 2 * key_dim
    q_scale = jnp.where(c * CONV_LANES < key_dim, HEAD_DIM ** -0.5, 1.0)
    pl.when(normalise)(lambda: body(q_scale))
    pl.when(jnp.logical_not(normalise))(lambda: body(None))


def _conv_prompt_kernel(halo_ref, cur_ref, w_ref, o_ref, *, tm, key_dim):
    t = pl.program_id(1)

    def body(scale):
        for r in range(0, tm, CONV_ROWS):
            if r == 0:
                prev = jnp.where(t > 0, halo_ref[...], 0.0)
            else:
                prev = cur_ref[r - CONV_HALO:r, :]
            cur = cur_ref[r:r + CONV_ROWS, :]
            pieces = _row_pieces(prev) + _row_pieces(cur)
            acc = cur * w_ref[CONV_WIDTH - 1:CONV_WIDTH, :]
            for d in range(1, CONV_WIDTH):
                tap = CONV_WIDTH - 1 - d
                back = jnp.concatenate(_shift_rows(pieces, d)[CONV_HALO // SUBLANES:], axis=0)
                acc = acc + back * w_ref[tap:tap + 1, :]
            o_ref[r:r + CONV_ROWS, :] = _conv_act(acc, scale)

    _conv_branches(pl.program_id(2), key_dim, body)


def _conv_prompt(x, conv_w, key_dim, b, t):
    m, c = x.shape
    tm = 1056
    assert t % tm == 0 and tm % CONV_ROWS == 0 and tm % CONV_HALO == 0 and key_dim % CONV_LANES == 0
    hb = tm // CONV_HALO
    nt = t // tm
    return pl.pallas_call(
        functools.partial(_conv_prompt_kernel, tm=tm, key_dim=key_dim),
        grid=(b, nt, c // CONV_LANES),
        in_specs=[pl.BlockSpec((CONV_HALO, CONV_LANES),
                               lambda i, j, l: (jnp.maximum((i * nt + j) * hb - 1, 0), l)),
                  pl.BlockSpec((tm, CONV_LANES), lambda i, j, l: (i * nt + j, l)),
                  pl.BlockSpec((CONV_WIDTH, CONV_LANES), lambda i, j, l: (0, l))],
        out_specs=pl.BlockSpec((tm, CONV_LANES), lambda i, j, l: (i * nt + j, l)),
        out_shape=jax.ShapeDtypeStruct((m, c), f32),
        compiler_params=_params(("parallel", "parallel", "parallel")),
        name="conv_prompt",
    )(x, x, conv_w)


def _conv_sample_kernel(*refs, key_dim):
    st_ref, x_ref, w_ref = refs[0], refs[1], refs[2]
    o_ref, ns_ref = refs[-2], refs[-1]

    def body(scale):
        acc = x_ref[...] * w_ref[CONV_WIDTH - 1:CONV_WIDTH, :]
        for i in range(CONV_WIDTH - 1):
            acc = acc + st_ref[:, i, :] * w_ref[i:i + 1, :]
        o_ref[...] = _conv_act(acc, scale)

    _conv_branches(pl.program_id(0), key_dim, body)
    for r in range(CONV_WIDTH - 2):
        ns_ref[:, r, :] = st_ref[:, r + 1, :]
    ns_ref[:, CONV_WIDTH - 2, :] = x_ref[...]


def _conv_sample(states, layer, x, act, row0, conv_w, key_dim, new_states):
    b = states.shape[1]
    c = x.shape[1]
    assert row0 % b == 0
    st_spec = pl.BlockSpec((None, b, CONV_WIDTH - 1, CONV_LANES), lambda l: (layer, 0, 0, l))
    row_spec = pl.BlockSpec((b, CONV_LANES), lambda l: (row0 // b, l))
    ns_shape, extra, extra_specs, alias = _stacked_out(new_states, layer, states.shape, states.dtype)
    return pl.pallas_call(
        functools.partial(_conv_sample_kernel, key_dim=key_dim),
        grid=(c // CONV_LANES,),
        in_specs=[st_spec, row_spec,
                  pl.BlockSpec((CONV_WIDTH, CONV_LANES), lambda l: (0, l)),
                  pl.BlockSpec(memory_space=pl.ANY)] + extra_specs,
        out_specs=[row_spec, st_spec],
        out_shape=[jax.ShapeDtypeStruct(act.shape, act.dtype), ns_shape],
        input_output_aliases={3: 0, **alias(4, 1)},
        compiler_params=_params(("parallel",)),
        name="conv_sample",
    )(states, x, conv_w, act, *extra)


def _gate_values(ba_ref, a_ref, dt_ref):
    x = ba_ref[...]
    xa = x + dt_ref[...]
    softplus = jnp.maximum(xa, 0.0) + jnp.log1p(jnp.exp(-jnp.abs(xa)))
    return jax.nn.sigmoid(x), -jnp.exp(a_ref[...]) * softplus


def _gate_chunk_kernel(ba_ref, a_ref, dt_ref, o_ref, *, n_heads):
    beta, g = _gate_values(ba_ref, a_ref, dt_ref)
    row = lax.broadcasted_iota(jnp.int32, (g.shape[0], 1), 0)
    g = jnp.where(row < jnp.where(pl.program_id(1) == 0, FRONT, 0), 0.0, g)
    in_chunk = row % CHUNK
    shift = 1
    while shift < CHUNK:
        g = g + jnp.where(in_chunk >= shift, pltpu.roll(g, shift, 0), 0.0)
        shift *= 2
    lane = lax.broadcasted_iota(jnp.int32, g.shape, 1)
    o_ref[...] = jnp.where(lane < n_heads, beta, g)


def _gates_chunked(ba, a_pad, dt_pad, b, t):
    w = ba.shape[1]
    rows = GATE_CHUNKS * CHUNK
    assert t % rows == 0 and rows >= FRONT
    nc = t // rows
    return pl.pallas_call(
        functools.partial(_gate_chunk_kernel, n_heads=w // 2),
        grid=(b, nc),
        in_specs=[pl.BlockSpec((rows, w), lambda i, j: (i * nc + j, 0)),
                  pl.BlockSpec((1, w), lambda i, j: (0, 0)),
                  pl.BlockSpec((1, w), lambda i, j: (0, 0))],
        out_specs=pl.BlockSpec((rows, w), lambda i, j: (i * nc + j, 0)),
        out_shape=jax.ShapeDtypeStruct((b * t, w), f32),
        compiler_params=_params(("parallel", "parallel")),
        name="gdn_gates_chunked",
    )(ba, a_pad, dt_pad)


def _gate_step_kernel(ba_ref, a_ref, dt_ref, beta_ref, g_ref, *, n_heads):
    beta, g = _gate_values(ba_ref, a_ref, dt_ref)
    rows = beta.shape[0]
    for h in range(n_heads):
        sl = slice(h * HEAD_DIM, (h + 1) * HEAD_DIM)
        beta_ref[:, sl] = jnp.broadcast_to(beta[:, h:h + 1], (rows, HEAD_DIM))
        g_ref[:, sl] = jnp.broadcast_to(g[:, n_heads + h:n_heads + h + 1], (rows, HEAD_DIM))


def _gates_step(ba, a_pad, dt_pad, row0, b):
    w = ba.shape[1]
    n_heads = w // 2
    wide = n_heads * HEAD_DIM
    assert row0 % b == 0
    return pl.pallas_call(
        functools.partial(_gate_step_kernel, n_heads=n_heads),
        grid=(1,),
        in_specs=[pl.BlockSpec((b, w), lambda i: (row0 // b, 0)),
                  pl.BlockSpec((1, w), lambda i: (0, 0)),
                  pl.BlockSpec((1, w), lambda i: (0, 0))],
        out_specs=[pl.BlockSpec((b, wide), lambda i: (0, 0)),
                   pl.BlockSpec((b, wide), lambda i: (0, 0))],
        out_shape=[jax.ShapeDtypeStruct((b, wide), f32), jax.ShapeDtypeStruct((b, wide), f32)],
        compiler_params=_params(("arbitrary",)),
        name="gdn_gates_step",
    )(ba, a_pad, dt_pad)


def _gate_params(a_log, dt_bias):
    zeros = jnp.zeros(a_log.shape, f32)
    w = 2 * a_log.shape[0]
    return (jnp.concatenate([zeros, a_log.astype(f32)]).reshape(1, w),
            jnp.concatenate([zeros, dt_bias.astype(f32)]).reshape(1, w))


def _dot_nt(a, b):
    return lax.dot_general(a, b, (((1,), (1,)), ((), ())), preferred_element_type=f32)


def _dot_tn(a, b):
    return lax.dot_general(a, b, (((0,), (0,)), ((), ())), preferred_element_type=f32)


def _dot16(a, b):
    return jnp.dot(a.astype(bf16), b.astype(bf16), preferred_element_type=f32)


def _stack(x, idx):
    return jnp.concatenate([x[:, i * HEAD_DIM:(i + 1) * HEAD_DIM] for i in idx], axis=0)


def _gdn_chunk_kernel(q_ref, k_ref, v_ref, bg_ref, gcr_ref, z_ref, ng_ref,
                      o_ref, sout_ref, s_ref):
    c = pl.program_id(2)

    @pl.when(c == 0)
    def _():
        s_ref[...] = jnp.zeros_like(s_ref)

    quads = range(QUADS_PER_STEP)
    heads = tuple(range(QUAD))
    khead = tuple(h // 2 for h in heads)
    rows = QUAD * CHUNK
    kq_w = QUAD // 2 * HEAD_DIM
    v_w = QUAD * HEAD_DIM

    def head_rows(h):
        return slice(h * CHUNK, (h + 1) * CHUNK)

    ri = lax.broadcasted_iota(jnp.int32, (rows, rows), 0)
    ci = lax.broadcasted_iota(jnp.int32, (rows, rows), 1)
    shift = CHUNK.bit_length() - 1
    same = lax.shift_right_logical(ri, shift) == lax.shift_right_logical(ci, shift)
    lower = same & (ri >= ci)
    strict = ri > ci

    bg = bg_ref[...]
    n_step = QUAD * QUADS_PER_STEP
    k16, kb16, q16, rhs, qd, kd16, decay, g_last = [], [], [], [], [], [], [], []
    for i in quads:
        kq = slice(i * kq_w, (i + 1) * kq_w)
        vs = slice(i * v_w, (i + 1) * v_w)
        kst = _stack(k_ref[:, kq], khead)
        qst = _stack(q_ref[:, kq], khead)
        cols = [i * QUAD + h for h in heads]
        bcol = [jnp.broadcast_to(bg[:, n:n + 1], (CHUNK, HEAD_DIM)) for n in cols]
        gcol = [jnp.broadcast_to(bg[:, n_step + n:n_step + n + 1], (CHUNK, HEAD_DIM)) for n in cols]
        bst = jnp.concatenate(bcol, axis=0)
        gst = jnp.concatenate(gcol, axis=0)
        last = [g[CHUNK - 1:CHUNK] for g in gcol]
        glast = jnp.concatenate([jnp.broadcast_to(r, (CHUNK, HEAD_DIM)) for r in last], axis=0)
        eg = jnp.exp(gst)
        kb = kst * bst
        k16.append(kst.astype(bf16))
        kb16.append(kb.astype(bf16))
        q16.append(qst.astype(bf16))
        rhs.append(jnp.concatenate([_stack(v_ref[:, vs], heads) * bst, kb * eg], axis=1))
        qd.append(qst * eg)
        kd16.append((kst * jnp.exp(glast - gst)).astype(bf16))
        diff = jnp.concatenate([gst] * (rows // HEAD_DIM), axis=1) - gcr_ref[i]
        decay.append(jnp.exp(jnp.where(lower, diff, -jnp.inf)))
        g_last.append([jnp.exp(r) for r in last])

    pw = [-jnp.where(strict, _dot_nt(kb16[i], k16[i]) * decay[i], 0.0) for i in quads]
    attn16 = [(_dot_nt(q16[i], k16[i]) * decay[i]).astype(bf16) for i in quads]

    sol = [rhs[i] + _dot16(pw[i], rhs[i]) for i in quads]
    for _ in range(NEUMANN_LEVELS - 1):
        pw16 = [p.astype(bf16) for p in pw]
        pw = [jnp.dot(p, p, preferred_element_type=f32) for p in pw16]
        sol = [sol[i] + _dot16(pw[i], sol[i]) for i in quads]

    x = [[_dot16(jnp.concatenate([sol[i][head_rows(h), HEAD_DIM:], qd[i][head_rows(h)]], axis=0),
                 s_ref[i * QUAD + h]) for h in heads] for i in quads]
    vn16 = [jnp.concatenate([sol[i][head_rows(h), :HEAD_DIM] - x[i][h][:CHUNK] for h in heads],
                            axis=0).astype(bf16) for i in quads]
    o = [jnp.concatenate([x[i][h][CHUNK:] for h in heads], axis=0)
         + jnp.dot(attn16[i], vn16[i], preferred_element_type=f32) for i in quads]
    for i in quads:
        for h in heads:
            s_ref[i * QUAD + h] = (s_ref[i * QUAD + h] * g_last[i][h]
                                   + _dot_tn(kd16[i][head_rows(h)], vn16[i][head_rows(h)]))

    for i in quads:
        on = o[i] * lax.rsqrt(jnp.mean(o[i] * o[i], axis=-1, keepdims=True) + EPS) * ng_ref[...]
        for h in heads:
            sl = slice(i * v_w + h * HEAD_DIM, i * v_w + (h + 1) * HEAD_DIM)
            o_ref[:, sl] = (on[head_rows(h)] * _silu(z_ref[:, sl])).astype(o_ref.dtype)

    @pl.when(c == pl.num_programs(2) - 1)
    def _():
        sout_ref[...] = s_ref[...]


def _gdn_chunked(qkv, bg, gc_rows, z, norm_g, batch):
    m, value_dim = z.shape
    key_dim = (qkv.shape[1] - value_dim) // 2
    n_heads = value_dim // HEAD_DIM
    n_chunks = bg.shape[1] // batch // CHUNK
    heads_per_step = QUAD * QUADS_PER_STEP
    n_groups = n_heads // heads_per_step
    kq_w = heads_per_step // 2 * HEAD_DIM
    v_w = heads_per_step * HEAD_DIM

    def row_spec(width, col0=0):
        off = col0 // width
        return pl.BlockSpec((CHUNK, width), lambda b, g, c: (b * n_chunks + c, g + off))

    return pl.pallas_call(
        _gdn_chunk_kernel,
        grid=(batch, n_groups, n_chunks),
        in_specs=[row_spec(kq_w), row_spec(kq_w, key_dim), row_spec(v_w, 2 * key_dim),
                  pl.BlockSpec((None, CHUNK, 2 * heads_per_step),
                               lambda b, g, c: (g, b * n_chunks + c, 0)),
                  pl.BlockSpec((None, None, QUADS_PER_STEP, 1, QUAD * CHUNK),
                               lambda b, g, c: (b, c, g, 0, 0)),
                  row_spec(v_w),
                  pl.BlockSpec((1, HEAD_DIM), lambda b, g, c: (0, 0))],
        out_specs=[row_spec(v_w),
                   pl.BlockSpec((None, heads_per_step, HEAD_DIM, HEAD_DIM),
                                lambda b, g, c: (b, g, 0, 0))],
        out_shape=[jax.ShapeDtypeStruct((m, value_dim), bf16),
                   jax.ShapeDtypeStruct((batch, n_heads, HEAD_DIM, HEAD_DIM), f32)],
        scratch_shapes=[pltpu.VMEM((heads_per_step, HEAD_DIM, HEAD_DIM), f32)],
        compiler_params=_params(("parallel", "parallel", "arbitrary")),
        name="gdn_chunked",
    )(qkv, qkv, qkv, bg, gc_rows, z, norm_g.reshape(1, HEAD_DIM))


def _gdn_step_kernel(*refs, n_heads):
    qt_ref, kt_ref, v_ref, beta_ref, g_ref, z_ref, ng_ref, s_ref = refs[:8]
    o_ref, sout_ref, kcol_ref, qcol_ref, eg_ref, oraw_ref = refs[-6:]
    n_k = qt_ref.shape[1]
    rep = n_heads // n_k
    for kh in range(n_k):
        kcol_ref[kh] = jnp.broadcast_to(kt_ref[:, kh:kh + 1], (HEAD_DIM, HEAD_DIM))
        qcol_ref[kh] = jnp.broadcast_to(qt_ref[:, kh:kh + 1], (HEAD_DIM, HEAD_DIM))
    eg_ref[...] = jnp.exp(g_ref[...])
    for h in range(n_heads):
        kcol = kcol_ref[h // rep]
        s = s_ref[h] * eg_ref[h:h + 1, :]
        kv = jnp.sum(kcol * s, axis=0, keepdims=True)
        delta = (v_ref[h:h + 1, :] - kv) * beta_ref[h:h + 1, :]
        s = s + kcol * delta
        sout_ref[h] = s
        oraw_ref[h:h + 1, :] = jnp.sum(qcol_ref[h // rep] * s, axis=0, keepdims=True)
    o = oraw_ref[...]
    on = o * lax.rsqrt(jnp.mean(o * o, axis=-1, keepdims=True) + EPS) * ng_ref[...]
    o_ref[...] = on * _silu(z_ref[...])


def _gdn_step(qt, kt, v3, beta3, g3, z3, norm_g, states, layer, new_states):
    b, n_heads, _ = v3.shape
    n_k = qt.shape[2]

    def head_spec():
        return pl.BlockSpec((None, n_heads, HEAD_DIM), lambda i: (i, 0, 0))

    def t_spec():
        return pl.BlockSpec((None, HEAD_DIM, n_k), lambda i: (i, 0, 0))

    s_spec = pl.BlockSpec((None, None, n_heads, HEAD_DIM, HEAD_DIM),
                          lambda i: (layer, i, 0, 0, 0))
    ns_shape, extra, extra_specs, alias = _stacked_out(new_states, layer, states.shape, f32)
    return pl.pallas_call(
        functools.partial(_gdn_step_kernel, n_heads=n_heads),
        grid=(b,),
        in_specs=[t_spec(), t_spec(), head_spec(), head_spec(), head_spec(), head_spec(),
                  pl.BlockSpec((1, HEAD_DIM), lambda i: (0, 0)), s_spec] + extra_specs,
        out_specs=[head_spec(), s_spec],
        out_shape=[jax.ShapeDtypeStruct((b, n_heads, HEAD_DIM), f32), ns_shape],
        scratch_shapes=[pltpu.VMEM((n_k, HEAD_DIM, HEAD_DIM), f32),
                        pltpu.VMEM((n_k, HEAD_DIM, HEAD_DIM), f32),
                        pltpu.VMEM((n_heads, HEAD_DIM), f32),
                        pltpu.VMEM((n_heads, HEAD_DIM), f32)],
        input_output_aliases=alias(8, 1),
        compiler_params=_params(("parallel",)),
        name="gdn_step",
    )(qt, kt, v3, beta3, g3, z3, norm_g.reshape(1, HEAD_DIM), states, *extra)


def _sequence_tails(x, batch, t, n):
    return jnp.stack([lax.slice_in_dim(x, (b + 1) * t - n, (b + 1) * t, axis=0)
                      for b in range(batch)])


def _pool_layer(h, rows_p, states, layer, new_states, w_in, w_grp, scale, w_out, norm_g, batch):
    e = w_grp.shape[1] * w_grp.shape[2]
    t = rows_p // batch
    xn = _rmsnorm(h, norm_g, bf16)
    u = _matmul(xn, w_in, layer, 0, e)
    z = _matmul(xn, w_in, layer, e, e)
    d = _pool_diff_prompt(u, batch, t)
    d, new_states = _pool_diff_sample(states, layer, u, d, rows_p, new_states)
    h = _out_project(h, _grouped_matmul_gate(d, w_grp, layer, scale, z), w_out, layer)
    pool_p = _sequence_tails(u, batch, t, POOL_CTX)
    return h, pool_p, new_states


def _gdn_layer(h, rows_p, conv_states, ssm_states, layer, new_conv, new_ssm, w_in, conv_w, a_log,
               dt_bias, head_g, w_out, norm_g, batch):
    value_dim = w_out.shape[1]
    n_heads = value_dim // HEAD_DIM
    conv_dim = conv_w.shape[1]
    key_dim = (conv_dim - value_dim) // 2
    n_k = key_dim // HEAD_DIM
    t = rows_p // batch
    bs = h.shape[0] - rows_p

    xn = _rmsnorm(h, norm_g, bf16)
    qkv = _matmul(xn, w_in, layer, 0, conv_dim)
    z = _matmul(xn, w_in, layer, conv_dim, value_dim)
    ba = _matmul(xn, w_in, layer, conv_dim + value_dim, 2 * n_heads)
    act = _conv_prompt(qkv, conv_w, key_dim, batch, t)
    act, new_conv = _conv_sample(conv_states, layer, qkv, act, rows_p, conv_w, key_dim, new_conv)
    a_pad, dt_pad = _gate_params(a_log, dt_bias)

    bg = _gates_chunked(ba, a_pad, dt_pad, batch, t)
    gc = bg[:, n_heads:]
    gc_rows = gc.reshape(batch, t // CHUNK, CHUNK, n_heads // QUAD, QUAD)
    gc_rows = jnp.transpose(gc_rows, (0, 1, 3, 4, 2)).reshape(
        batch, t // CHUNK, n_heads // QUAD, 1, QUAD * CHUNK)
    n_step = QUAD * QUADS_PER_STEP
    bg = jnp.concatenate([bg[:, :n_heads].reshape(-1, n_heads // n_step, n_step),
                          gc.reshape(-1, n_heads // n_step, n_step)], axis=2)
    o, ssm_p = _gdn_chunked(act, jnp.transpose(bg, (1, 0, 2)), gc_rows, z, head_g, batch)
    conv_p = _sequence_tails(qkv, batch, t, CONV_WIDTH - 1)

    beta_b, g_b = _gates_step(ba, a_pad, dt_pad, rows_p, bs)
    act_s = act[rows_p:]
    qt = jnp.transpose(act_s[:, :key_dim].reshape(bs, n_k, HEAD_DIM), (0, 2, 1))
    kt = jnp.transpose(act_s[:, key_dim:2 * key_dim].reshape(bs, n_k, HEAD_DIM), (0, 2, 1))
    per_head = (bs, n_heads, HEAD_DIM)
    o_s, new_ssm = _gdn_step(qt, kt, act_s[:, 2 * key_dim:].reshape(per_head),
                             beta_b.reshape(per_head), g_b.reshape(per_head),
                             z[rows_p:].reshape(per_head), head_g, ssm_states, layer, new_ssm)
    o = lax.dynamic_update_slice(o, o_s.reshape(bs, value_dim).astype(bf16), (rows_p, 0))
    h = _out_project(h, o, w_out, layer)
    return h, conv_p, ssm_p, new_conv, new_ssm


def kernel(x_prompt, x_sample, state_pool, state_conv, state_ssm, meta_tokens, norm_g, final_norm_g, pool_w_in, pool_w_grp, pool_scale, pool_w_out, gdn_w_in, gdn_conv_w, gdn_A_log, gdn_dt_bias, gdn_norm_g, gdn_w_out):
    dt = x_prompt.dtype
    batch, seq, d = x_prompt.shape
    depth = norm_g.shape[0]
    meta = jnp.broadcast_to(meta_tokens.astype(dt)[None], (batch, N_META, d))
    hp = jnp.concatenate([jnp.zeros((batch, FRONT, d), dt), meta, x_prompt], axis=1)
    t = hp.shape[1]
    rows_p = batch * t
    h = jnp.concatenate([hp.reshape(rows_p, d), x_sample.reshape(x_sample.shape[0], d)], axis=0)
    pool_p, conv_p, ssm_p = [], [], []
    pool_s = conv_s = ssm_s = None
    for i in range(depth):
        j = i // 2
        if i % 2 == 0:
            h, pp, pool_s = _pool_layer(h, rows_p, state_pool, j, pool_s, pool_w_in, pool_w_grp,
                                        pool_scale[j], pool_w_out, norm_g[i], batch)
            pool_p.append(pp)
        else:
            h, cp, sp, conv_s, ssm_s = _gdn_layer(
                h, rows_p, state_conv, state_ssm, j, conv_s, ssm_s, gdn_w_in, gdn_conv_w[j],
                gdn_A_log[j], gdn_dt_bias[j], gdn_norm_g[j], gdn_w_out, norm_g[i], batch)
            conv_p.append(cp)
            ssm_p.append(sp)
    y_prompt = _final_norm_prompt(h, final_norm_g, batch, t, seq)
    y_sample = _rmsnorm(h[rows_p:], final_norm_g, dt).reshape(x_sample.shape)
    return (y_prompt, y_sample, jnp.stack(pool_p), jnp.stack(conv_p), jnp.stack(ssm_p),
            pool_s, conv_s, ssm_s)
```
